```python
import math
import jax, jax.numpy as jnp
from jax import lax
import numpy as np

D_MODEL = 1024
BATCH = 8
SEQ = 2048
DEPTH = 2
DEC_BATCH = 4
DEC_SEQ = 4096
PAST_LEN = 128

N_HEADS = 12
HEAD_DIM = 64
MIX_WIDTH = N_HEADS * HEAD_DIM
DILATED_GROUPS = ((128, 1), (512, 4), (2048, 16))
HEADS_PER_GROUP = N_HEADS // len(DILATED_GROUPS)
N_KV_HEADS = 4
Q_PER_KV = N_HEADS // N_KV_HEADS
KV_WIDTH = N_KV_HEADS * HEAD_DIM
LOCAL_HALF_WINDOW = 128
N_BUCKETS = 32
REL_MAX_DISTANCE = 1024
N_EXPERTS = 16
EC_CAPACITY_FACTOR = 2
D_EXPERT = 1024
N_MIXERS = 2
N_A_LAYERS = (DEPTH + 1) // 2
N_B_LAYERS = DEPTH // 2
RMS_EPS = 1e-6
NEG_INF = -1e30

kernel_name = "hybrid_dilated_local_ec_encoder"


def rms_norm(x, gain):
    xf = x.astype(jnp.float32)
    y = xf * lax.rsqrt(jnp.mean(xf * xf, axis=-1, keepdims=True) + RMS_EPS)
    return (y * gain.astype(jnp.float32)).astype(x.dtype)


def t5_bucket(rel):
    half = N_BUCKETS // 2
    max_exact = half // 2
    n = -rel
    base = np.where(n < 0, half, 0)
    n = np.abs(n)
    large = max_exact + (np.log(np.maximum(n, 1) / max_exact)
                         / math.log(REL_MAX_DISTANCE / max_exact) * (half - max_exact)).astype(np.int32)
    large = np.minimum(large, half - 1)
    return (base + np.where(n < max_exact, n, large)).astype(np.int32)


def band_offsets(block):
    return np.arange(3 * block)[None, :] - block - np.arange(block)[:, None]


def banded_attention(q, k, v, half_window, bias, sink=None):
    B, K, G, n, D = q.shape
    qb_size = half_window
    nb = -(-n // qb_size)
    pad = nb * qb_size - n
    qb = jnp.pad(q, ((0, 0), (0, 0), (0, 0), (0, pad), (0, 0))).reshape(B, K, G, nb, qb_size, D)

    def key_blocks(t):
        tp = jnp.pad(t, ((0, 0), (0, 0), (qb_size, qb_size + pad), (0, 0))).reshape(B, K, nb + 2, qb_size, D)
        return jnp.concatenate([tp[:, :, :-2], tp[:, :, 1:-1], tp[:, :, 2:]], axis=3)

    kb, vb = key_blocks(k), key_blocks(v)
    rel = band_offsets(qb_size)
    key_pos = np.arange(nb)[:, None] * qb_size + np.arange(3 * qb_size)[None, :] - qb_size
    mask = (np.abs(rel) <= half_window)[None] & ((key_pos >= 0) & (key_pos < n))[:, None, :]

    s = jnp.einsum('bkgnqd,bkncd->bkgnqc', qb, kb, preferred_element_type=jnp.float32) * (D ** -0.5)
    s = jnp.where(mask, s + bias[None, :, :, None], NEG_INF)
    lse = jax.nn.logsumexp(s, axis=-1)
    if sink is not None:
        lse = jnp.logaddexp(lse, sink.astype(jnp.float32)[None, :, :, None, None])
    p = jnp.exp(s - lse[..., None])
    out = jnp.einsum('bkgnqc,bkncd->bkgnqd', p.astype(v.dtype), vb)
    out = out.reshape(B, K, G, nb * qb_size, D)[:, :, :, :n]
    lse = lse.reshape(B, K, G, nb * qb_size)[:, :, :, :n]
    return out, lse


def dilated_attention(h, w_in, q_gain, k_gain, w_out, rel_bias):
    B, L, _ = h.shape
    qkv = (h @ w_in).reshape(B, L, 3, N_HEADS, HEAD_DIM)
    q = rms_norm(qkv[:, :, 0], q_gain)
    k = rms_norm(qkv[:, :, 1], k_gain)
    v = qkv[:, :, 2]
    outs, lses = [], []
    for g, (window, dil) in enumerate(DILATED_GROUPS):
        heads = slice(g * HEADS_PER_GROUP, (g + 1) * HEADS_PER_GROUP)
        half = window // (2 * dil)
        n = L // dil

        def fold(t):
            return t.reshape(B, n, dil, HEADS_PER_GROUP, HEAD_DIM).transpose(0, 2, 3, 1, 4).reshape(
                B * dil, HEADS_PER_GROUP, n, HEAD_DIM)

        qg, kg, vg = fold(q[:, :, heads]), fold(k[:, :, heads]), fold(v[:, :, heads])
        bucket = t5_bucket(band_offsets(half) * dil)
        bias = rel_bias[bucket, heads].astype(jnp.float32).transpose(2, 0, 1)[:, None]
        o, lse = banded_attention(qg[:, :, None], kg, vg, half, bias)
        o = o[:, :, 0].reshape(B, dil, HEADS_PER_GROUP, n, HEAD_DIM).transpose(0, 3, 1, 2, 4).reshape(
            B, L, HEADS_PER_GROUP, HEAD_DIM)
        lse = lse[:, :, 0].reshape(B, dil, HEADS_PER_GROUP, n).transpose(0, 3, 1, 2).reshape(
            B, L, HEADS_PER_GROUP)
        outs.append(o)
        lses.append(lse)
    alpha = jax.nn.softmax(jnp.stack(lses), axis=0)
    outs = jnp.stack(outs)
    mixed = (alpha[..., None].astype(outs.dtype) * outs).transpose(1, 2, 0, 3, 4).reshape(B, L, MIX_WIDTH)
    return mixed @ w_out


def local_gqa_attention(h, w_in, q_gain, k_gain, sink, w_out, rel_bias):
    B, L, _ = h.shape
    proj = h @ w_in
    q = proj[..., :MIX_WIDTH].reshape(B, L, N_KV_HEADS, Q_PER_KV, HEAD_DIM)
    k = proj[..., MIX_WIDTH:MIX_WIDTH + KV_WIDTH].reshape(B, L, N_KV_HEADS, HEAD_DIM)
    v = proj[..., MIX_WIDTH + KV_WIDTH:].reshape(B, L, N_KV_HEADS, HEAD_DIM)
    q = rms_norm(q, q_gain).transpose(0, 2, 3, 1, 4)
    k = rms_norm(k, k_gain).transpose(0, 2, 1, 3)
    v = v.transpose(0, 2, 1, 3)
    half = LOCAL_HALF_WINDOW
    bucket = t5_bucket(band_offsets(half))
    bias = rel_bias[bucket].astype(jnp.float32).transpose(2, 0, 1).reshape(N_KV_HEADS, Q_PER_KV, half, 3 * half)
    o, _ = banded_attention(q, k, v, half, bias, sink.reshape(N_KV_HEADS, Q_PER_KV))
    o = o.transpose(0, 3, 1, 2, 4).reshape(B, L, MIX_WIDTH)
    return o @ w_out


def expert_choice_ffn(h, w_router, w_gate, w_up, w_down):
    B, L, Dm = h.shape
    T = B * L
    cap = EC_CAPACITY_FACTOR * T // N_EXPERTS
    tokens = h.reshape(T, Dm)
    affinity = jax.nn.softmax(
        jnp.einsum('td,de->te', tokens, w_router, preferred_element_type=jnp.float32), axis=-1)
    gate, idx = lax.top_k(affinity.T, cap)
    xe = tokens[idx]
    hid = jax.nn.silu(jnp.einsum('ecd,edf->ecf', xe, w_gate)) * jnp.einsum('ecd,edf->ecf', xe, w_up)
    ye = jnp.einsum('ecf,efd->ecd', hid, w_down) * gate[..., None].astype(hid.dtype)
    out = jnp.zeros_like(tokens).at[idx.reshape(-1)].add(ye.reshape(-1, Dm).astype(tokens.dtype))
    return out.reshape(B, L, Dm)


def encoder_trunk(x, rel_bias, norm_mix, norm_ffn, a_w_in, a_q_gain, a_k_gain, a_w_out,
                  b_w_in, b_q_gain, b_k_gain, b_sink, b_w_out,
                  moe_router, moe_w_gate, moe_w_up, moe_w_down):
    for i in range(DEPTH):
        h = rms_norm(x, norm_mix[i])
        j = i // N_MIXERS
        if i % N_MIXERS == 0:
            x = x + dilated_attention(h, a_w_in[j], a_q_gain[j], a_k_gain[j], a_w_out[j], rel_bias)
        else:
            x = x + local_gqa_attention(h, b_w_in[j], b_q_gain[j], b_k_gain[j], b_sink[j], b_w_out[j], rel_bias)
        h = rms_norm(x, norm_ffn[i])
        x = x + expert_choice_ffn(h, moe_router[i], moe_w_gate[i], moe_w_up[i], moe_w_down[i])
    return x


def setup_inputs(seed: int = 0) -> dict:
    key = jax.random.key(seed)
    ks = jax.random.split(key, 20)
    f32 = jnp.float32

    def nrm(k, shape, scale):
        return jax.random.normal(k, shape, f32) * scale

    return {
        "x_prompt": nrm(ks[0], (BATCH, SEQ, D_MODEL), 1.0),
        "x_sample": nrm(ks[1], (DEC_BATCH, DEC_SEQ, D_MODEL), 1.0),
        "rel_bias": nrm(ks[2], (N_BUCKETS, N_HEADS), 0.5),
        "norm_mix": 1.0 + nrm(ks[3], (DEPTH, D_MODEL), 0.02),
        "norm_ffn": 1.0 + nrm(ks[4], (DEPTH, D_MODEL), 0.02),
        "a_w_in": nrm(ks[5], (N_A_LAYERS, D_MODEL, 3 * MIX_WIDTH), D_MODEL ** -0.5),
        "a_q_gain": 1.0 + nrm(ks[6], (N_A_LAYERS, HEAD_DIM), 0.02),
        "a_k_gain": 1.0 + nrm(ks[7], (N_A_LAYERS, HEAD_DIM), 0.02),
        "a_w_out": nrm(ks[8], (N_A_LAYERS, MIX_WIDTH, D_MODEL), MIX_WIDTH ** -0.5),
        "b_w_in": nrm(ks[9], (N_B_LAYERS, D_MODEL, MIX_WIDTH + 2 * KV_WIDTH), D_MODEL ** -0.5),
        "b_q_gain": 1.0 + nrm(ks[10], (N_B_LAYERS, HEAD_DIM), 0.02),
        "b_k_gain": 1.0 + nrm(ks[11], (N_B_LAYERS, HEAD_DIM), 0.02),
        "b_sink": nrm(ks[12], (N_B_LAYERS, N_HEADS), 1.0),
        "b_w_out": nrm(ks[13], (N_B_LAYERS, MIX_WIDTH, D_MODEL), MIX_WIDTH ** -0.5),
        "moe_router": nrm(ks[14], (DEPTH, D_MODEL, N_EXPERTS), D_MODEL ** -0.5),
        "moe_w_gate": nrm(ks[15], (DEPTH, N_EXPERTS, D_MODEL, D_EXPERT), D_MODEL ** -0.5),
        "moe_w_up": nrm(ks[16], (DEPTH, N_EXPERTS, D_MODEL, D_EXPERT), D_MODEL ** -0.5),
        "moe_w_down": nrm(ks[17], (DEPTH, N_EXPERTS, D_EXPERT, D_MODEL), D_EXPERT ** -0.5),
    }


def reference(x_prompt, x_sample, rel_bias, norm_mix, norm_ffn, a_w_in, a_q_gain, a_k_gain, a_w_out,
              b_w_in, b_q_gain, b_k_gain, b_sink, b_w_out, moe_router, moe_w_gate, moe_w_up, moe_w_down):
    y_prompt = encoder_trunk(x_prompt, rel_bias, norm_mix, norm_ffn, a_w_in, a_q_gain, a_k_gain, a_w_out,
                             b_w_in, b_q_gain, b_k_gain, b_sink, b_w_out,
                             moe_router, moe_w_gate, moe_w_up, moe_w_down)
    y_sample = encoder_trunk(x_sample, rel_bias, norm_mix, norm_ffn, a_w_in, a_q_gain, a_k_gain, a_w_out,
                             b_w_in, b_q_gain, b_k_gain, b_sink, b_w_out,
                             moe_router, moe_w_gate, moe_w_up, moe_w_down)
    return (y_prompt, y_sample)
```

```python
import functools
import math

import numpy as np
import jax
import jax.numpy as jnp
from jax import lax
from jax.experimental import pallas as pl
from jax.experimental.pallas import tpu as pltpu

F32 = jnp.float32
BF16 = jnp.bfloat16
I32 = jnp.int32

D_MODEL = 1024
N_HEADS = 12
HEAD_DIM = 64
MIX_WIDTH = N_HEADS * HEAD_DIM
DILATED_GROUPS = ((128, 1), (512, 4), (2048, 16))
N_KV_HEADS = 4
Q_PER_KV = N_HEADS // N_KV_HEADS
KV_WIDTH = N_KV_HEADS * HEAD_DIM
LOCAL_HALF_WINDOW = 128
N_BUCKETS = 32
REL_MAX_DISTANCE = 1024
N_EXPERTS = 16
EC_CAPACITY_FACTOR = 2
D_EXPERT = 1024
DEPTH = 2
RMS_EPS = 1e-6
NEG_INF = -1e30

LANES = 128
HEADS_PER_CHUNK = 4
CHUNK = HEADS_PER_CHUNK * HEAD_DIM
VMEM_LIMIT = 56 * 1024 * 1024

PROJ_TM = 512
ATTN_TQ = 512
FFN_TC = 256
ROW_BLK = 256
SEG_TT = 256
SEG_CH = 256
AUG = LANES


def _cparams(sem):
    return pltpu.CompilerParams(dimension_semantics=sem, vmem_limit_bytes=VMEM_LIMIT)


def _proj_body(x_ref, g_ref, w_ref, bd_ref, qk_ref, o_ref, *, n_norm_chunks):
    x = x_ref[...]
    ms = jnp.mean(x * x, axis=-1, keepdims=True)
    h = (x * lax.rsqrt(ms + RMS_EPS) * g_ref[...]).astype(BF16)
    acc = jnp.dot(h, w_ref[...], preferred_element_type=F32)
    bd = bd_ref[...]
    for c in range(acc.shape[1] // CHUNK):
        sl = slice(c * CHUNK, (c + 1) * CHUNK)
        a = acc[:, sl]
        if c < n_norm_chunks:
            a2 = a * a
            hi = a2.astype(BF16)
            lo = (a2 - hi.astype(F32)).astype(BF16)
            ss = jnp.dot(hi, bd, preferred_element_type=F32) + jnp.dot(lo, bd, preferred_element_type=F32)
            a = a * lax.rsqrt(ss * (1.0 / HEAD_DIM) + RMS_EPS) * qk_ref[:, sl]
        o_ref[:, sl] = a.astype(o_ref.dtype)


def _project(x2d, gain, w_bf, bd, qk_gain, n_norm_chunks):
    t, d = x2d.shape
    n = w_bf.shape[1]
    tm = PROJ_TM
    return pl.pallas_call(
        functools.partial(_proj_body, n_norm_chunks=n_norm_chunks),
        grid=(t // tm,),
        in_specs=[
            pl.BlockSpec((tm, d), lambda i: (i, 0)),
            pl.BlockSpec((1, d), lambda i: (0, 0)),
            pl.BlockSpec((d, n), lambda i: (0, 0)),
            pl.BlockSpec((CHUNK, CHUNK), lambda i: (0, 0)),
            pl.BlockSpec((1, n), lambda i: (0, 0)),
        ],
        out_specs=pl.BlockSpec((tm, n), lambda i: (i, 0)),
        out_shape=jax.ShapeDtypeStruct((t, n), BF16),
        compiler_params=_cparams(("parallel",)),
        name="proj",
    )(x2d, gain, w_bf, bd, qk_gain)


def _attn_body(*refs, qb, wk, n, gq, nblk, has_sink, want_lse):
    q_ref, k_ref, v_ref, bias_ref = refs[:4]
    pos = 4
    sink_ref = None
    if has_sink:
        sink_ref = refs[pos]
        pos += 1
    o_ref = refs[pos]
    lse_ref = refs[pos + 1] if want_lse else None

    jt = pl.program_id(2)
    lane_head = lax.broadcasted_iota(I32, (1, CHUNK), 1) // HEAD_DIM
    masks = [lane_head == h for h in range(HEADS_PER_CHUNK)]

    def block(blk, carry):
        i = jt * nblk + blk
        ws = jnp.clip((i - 1) * qb, 0, n - wk)
        var = i - ws // qb
        ws = pl.multiple_of(ws, qb)
        r0 = pl.multiple_of(blk * qb, qb)
        kb = k_ref[0, pl.ds(ws, wk), :]
        vb = v_ref[0, pl.ds(ws, wk), :]
        for g in range(gq):
            cs = slice(g * CHUNK, (g + 1) * CHUNK)
            qg = q_ref[0, pl.ds(r0, qb), cs]
            zero = jnp.zeros_like(qg)
            q4 = jnp.concatenate([jnp.where(m, qg, zero) for m in masks], axis=0)
            s = lax.dot_general(q4, kb, (((1,), (1,)), ((), ())), preferred_element_type=F32)
            s = s + bias_ref[var, g]
            m = jnp.max(s, axis=-1, keepdims=True)
            if has_sink:
                snk = sink_ref[g]
                m = jnp.maximum(m, snk)
            e = jnp.exp(s - m)
            l = jnp.sum(e, axis=-1, keepdims=True)
            if has_sink:
                l = l + jnp.exp(snk - m)
            p = (e * (1.0 / l)).astype(BF16)
            pv = jnp.dot(p, vb, preferred_element_type=F32)
            o = jnp.zeros((qb, CHUNK), F32)
            for h in range(HEADS_PER_CHUNK):
                o = jnp.where(masks[h], pv[h * qb:(h + 1) * qb], o)
            o_ref[0, pl.ds(r0, qb), cs] = o.astype(o_ref.dtype)
            if want_lse:
                lse = m + jnp.log(l)
                lo = jnp.zeros((qb, CHUNK), F32)
                for h in range(HEADS_PER_CHUNK):
                    lo = jnp.where(masks[h], lse[h * qb:(h + 1) * qb], lo)
                lse_ref[0, pl.ds(r0, qb), cs] = lo
        return carry

    lax.fori_loop(0, nblk, block, 0)


def _t5_bucket(rel):
    half = N_BUCKETS // 2
    max_exact = half // 2
    nn = -rel
    base = np.where(nn < 0, half, 0)
    nn = np.abs(nn)
    large = max_exact + (np.log(np.maximum(nn, 1) / max_exact)
                         / math.log(REL_MAX_DISTANCE / max_exact) * (half - max_exact)).astype(np.int32)
    large = np.minimum(large, half - 1)
    return (base + np.where(nn < max_exact, nn, large)).astype(np.int32)


def _bias_table(rel_bias, heads, qb, wk, dil):
    var = np.arange(3)[:, None, None]
    rel = np.arange(wk)[None, None, :] - np.arange(qb)[None, :, None] - var * qb
    bucket = _t5_bucket(rel * dil)
    ok = np.abs(rel) <= qb
    b = rel_bias.astype(F32)[bucket][..., np.asarray(heads)]
    b = jnp.where(ok[..., None], b, NEG_INF)
    nh = len(heads)
    b = b.transpose(0, 3, 1, 2).reshape(3, nh // HEADS_PER_CHUNK, HEADS_PER_CHUNK * qb, wk)
    return b


def _banded_attention(qkv, bias, sink, *, batch, n, r, qb, gq, q_col, k_col, v_col, width_in, width_out,
                      out_dtype, want_lse):
    wk = min(3 * qb, n)
    tq = min(ATTN_TQ, n)
    nblk = tq // qb
    qw = gq * CHUNK
    in_specs = [
        pl.BlockSpec((1, tq, qw), lambda b, p, j: (b, j, p * (width_in // qw) + q_col)),
        pl.BlockSpec((1, n, CHUNK), lambda b, p, j: (b, 0, p * (width_in // CHUNK) + k_col)),
        pl.BlockSpec((1, n, CHUNK), lambda b, p, j: (b, 0, p * (width_in // CHUNK) + v_col)),
        pl.BlockSpec(bias.shape, lambda b, p, j: (0, 0, 0, 0)),
    ]
    args = [qkv, qkv, qkv, bias]
    if sink is not None:
        in_specs.append(pl.BlockSpec(sink.shape, lambda b, p, j: (0, 0, 0)))
        args.append(sink)
    o_spec = pl.BlockSpec((1, tq, qw), lambda b, p, j: (b, j, p * (width_out // qw)))
    o_shape = jax.ShapeDtypeStruct((batch, n, r * width_out), out_dtype)
    out_specs, out_shape = [o_spec], [o_shape]
    if want_lse:
        out_specs.append(o_spec)
        out_shape.append(jax.ShapeDtypeStruct((batch, n, r * width_out), F32))
    return pl.pallas_call(
        functools.partial(_attn_body, qb=qb, wk=wk, n=n, gq=gq, nblk=nblk, has_sink=sink is not None,
                          want_lse=want_lse),
        grid=(batch, r, n // tq),
        in_specs=in_specs,
        out_specs=out_specs,
        out_shape=out_shape,
        compiler_params=_cparams(("parallel", "parallel", "arbitrary")),
        name="attn",
    )(*args)


def _outproj_body(*refs, n_mix):
    if n_mix:
        o_refs, lse_refs = refs[:n_mix], refs[n_mix:2 * n_mix]
        x_ref, w_ref, g_ref, rh_ref, rl_ref, x1_ref, h_ref, aff_ref = refs[2 * n_mix:]
        ls = [r[...] for r in lse_refs]
        mx = functools.reduce(jnp.maximum, ls)
        es = [jnp.exp(l - mx) for l in ls]
        inv = 1.0 / functools.reduce(lambda a, b: a + b, es)
        mixed = jnp.concatenate([(es[g] * inv) * o_refs[g][...] for g in range(n_mix)], axis=1).astype(BF16)
    else:
        o_ref, x_ref, w_ref, g_ref, rh_ref, rl_ref, x1_ref, h_ref, aff_ref = refs
        mixed = o_ref[...]
    x1 = x_ref[...] + jnp.dot(mixed, w_ref[...], preferred_element_type=F32)
    x1_ref[...] = x1
    ms = jnp.mean(x1 * x1, axis=-1, keepdims=True)
    h = x1 * lax.rsqrt(ms + RMS_EPS) * g_ref[...]
    h_ref[...] = h
    h_hi = h.astype(BF16)
    h_lo = (h - h_hi.astype(F32)).astype(BF16)
    rh = rh_ref[...]
    logits = (jnp.dot(h_hi, rh, preferred_element_type=F32) + jnp.dot(h_lo, rh, preferred_element_type=F32)
              + jnp.dot(h_hi, rl_ref[...], preferred_element_type=F32))
    lt = logits.T[:N_EXPERTS]
    mx = jnp.max(lt, axis=0, keepdims=True)
    e = jnp.exp(lt - mx)
    aff_ref[...] = e / jnp.sum(e, axis=0, keepdims=True)


def _out_project(o_list, lse_list, x2d, w_bf, gain, r_hi, r_lo):
    t, d = x2d.shape
    tm = PROJ_TM
    n_mix = len(lse_list)
    args = list(o_list) + list(lse_list)
    in_specs = [pl.BlockSpec((tm, a.shape[1]), lambda i: (i, 0)) for a in args]
    in_specs += [
        pl.BlockSpec((tm, d), lambda i: (i, 0)),
        pl.BlockSpec((w_bf.shape[0], d), lambda i: (0, 0)),
        pl.BlockSpec((1, d), lambda i: (0, 0)),
        pl.BlockSpec((d, LANES), lambda i: (0, 0)),
        pl.BlockSpec((d, LANES), lambda i: (0, 0)),
    ]
    args += [x2d, w_bf, gain, r_hi, r_lo]
    return pl.pallas_call(
        functools.partial(_outproj_body, n_mix=n_mix),
        grid=(t // tm,),
        in_specs=in_specs,
        out_specs=[
            pl.BlockSpec((tm, d), lambda i: (i, 0)),
            pl.BlockSpec((tm, d), lambda i: (i, 0)),
            pl.BlockSpec((N_EXPERTS, tm), lambda i: (0, i)),
        ],
        out_shape=[
            jax.ShapeDtypeStruct((t, d), F32),
            jax.ShapeDtypeStruct((t, d), F32),
            jax.ShapeDtypeStruct((N_EXPERTS, t), F32),
        ],
        compiler_params=_cparams(("parallel",)),
        name="outproj",
    )(*args)


def _count(m):
    return jnp.sum(jnp.sum(m.astype(F32), axis=0, keepdims=True), axis=1, keepdims=True)


def _split3(x):
    hi = x.astype(BF16)
    r1 = x - hi.astype(F32)
    mid = r1.astype(BF16)
    lo = (r1 - mid.astype(F32)).astype(BF16)
    return hi, mid, lo


def _route_body(aff_ref, idx_ref, gate_ref, pos_ref, base_ref, mask_scr, pos_scr, *, cap, nchunk):
    ne = N_EXPERTS
    cpos = LANES
    sub_i = lax.broadcasted_iota(I32, (nchunk, cpos), 0)
    lane_i = lax.broadcasted_iota(I32, (nchunk, cpos), 1)
    tok = sub_i * cpos + lane_i
    capf = jnp.float32(cap)

    def bits(e):
        return pltpu.bitcast(aff_ref[e], I32)

    def val_step(it, taus):
        bit = jnp.left_shift(jnp.int32(1), 30 - it)
        out = []
        for e in range(ne):
            cand = taus[e] | bit
            cnt = _count(bits(e) >= cand)
            out.append(jnp.where(cnt >= capf, cand, taus[e]))
        return tuple(out)

    taus = lax.fori_loop(0, 31, val_step, tuple(jnp.zeros((1, 1), I32) for _ in range(ne)))
    need = [capf - _count(bits(e) > taus[e]) for e in range(ne)]

    tbits = int(math.log2(nchunk * cpos))

    def tie_step(it, vs):
        bit = jnp.left_shift(jnp.int32(1), tbits - 1 - it)
        out = []
        for e in range(ne):
            cand = vs[e] | bit
            cnt = _count((bits(e) == taus[e]) & (tok < cand))
            out.append(jnp.where(cnt < need[e], cand, vs[e]))
        return tuple(out)

    vs = lax.fori_loop(0, tbits, tie_step, tuple(jnp.zeros((1, 1), I32) for _ in range(ne)))

    ksum = jnp.zeros((nchunk, cpos), F32)
    for e in range(ne):
        b = bits(e)
        m = ((b > taus[e]) | ((b == taus[e]) & (tok <= vs[e]))).astype(F32)
        mask_scr[e] = m
        pos_scr[e] = ksum
        ksum = ksum + m
    tri_u = (lax.broadcasted_iota(I32, (cpos, cpos), 0)
             <= lax.broadcasted_iota(I32, (cpos, cpos), 1)).astype(BF16)
    cs_i = lax.broadcasted_iota(I32, (nchunk, nchunk), 0)
    cl_i = lax.broadcasted_iota(I32, (nchunk, nchunk), 1)
    tri_ls = (cl_i < cs_i).astype(BF16)
    tri_li = (cl_i <= cs_i).astype(BF16)
    kin = jnp.dot(ksum.astype(BF16), tri_u, preferred_element_type=F32)
    ktot = jnp.broadcast_to(kin[:, cpos - 1:cpos], (nchunk, cpos))
    khi = jnp.floor(ktot * (1.0 / 256.0))
    klo = ktot - 256.0 * khi
    kstart = (256.0 * jnp.dot(tri_ls, khi.astype(BF16), preferred_element_type=F32)
              + jnp.dot(tri_ls, klo.astype(BF16), preferred_element_type=F32))
    base = kstart + kin - ksum
    base_ref[...] = base

    s_row = lax.broadcasted_iota(I32, (1, cap), 1).astype(F32)
    c_col = lax.broadcasted_iota(I32, (nchunk, cap), 0).astype(F32)
    j_col = lax.broadcasted_iota(I32, (cpos, cap), 0).astype(F32)
    reps = cap // cpos
    for e in range(ne):
        m = mask_scr[e]
        posv = pos_scr[e] + base
        lin = jnp.dot(m.astype(BF16), tri_u, preferred_element_type=F32)
        tot = jnp.broadcast_to(lin[:, cpos - 1:cpos], (nchunk, cpos))
        cend = jnp.dot(tri_li, tot.astype(BF16), preferred_element_type=F32)
        cstart = cend - tot
        cend_w = jnp.concatenate([cend] * reps, axis=1)
        cstart_w = jnp.concatenate([cstart] * reps, axis=1)
        cidx = jnp.sum((cend_w <= s_row).astype(F32), axis=0, keepdims=True)
        oh = c_col == cidx
        ohb = oh.astype(BF16)
        cst = jnp.sum(jnp.where(oh, cstart_w, 0.0), axis=0, keepdims=True)
        target = s_row - cst
        g = jnp.dot(lin.T.astype(BF16), ohb, preferred_element_type=F32)
        jidx = jnp.sum((g <= target).astype(F32), axis=0, keepdims=True)
        ohj = j_col == jidx
        idx_ref[e:e + 1, :] = (cidx * float(cpos) + jidx).astype(I32)
        a_hi, a_mid, a_lo = _split3(aff_ref[e].T)
        ga = (jnp.dot(a_hi, ohb, preferred_element_type=F32) + jnp.dot(a_mid, ohb, preferred_element_type=F32)
              + jnp.dot(a_lo, ohb, preferred_element_type=F32))
        gate_ref[e:e + 1, :] = jnp.sum(jnp.where(ohj, ga, 0.0), axis=0, keepdims=True)
        pt = posv.T
        p_hi = jnp.floor(pt * (1.0 / 256.0))
        p_lo = pt - 256.0 * p_hi
        gp = (256.0 * jnp.dot(p_hi.astype(BF16), ohb, preferred_element_type=F32)
              + jnp.dot(p_lo.astype(BF16), ohb, preferred_element_type=F32))
        pos_ref[e:e + 1, :] = jnp.sum(jnp.where(ohj, gp, 0.0), axis=0, keepdims=True).astype(I32)


def _route(aff_t, cap):
    ne, t = aff_t.shape
    nchunk = t // LANES
    aff3 = aff_t.reshape(ne, nchunk, LANES)
    return pl.pallas_call(
        functools.partial(_route_body, cap=cap, nchunk=nchunk),
        in_specs=[pl.BlockSpec(memory_space=pltpu.VMEM)],
        out_specs=[pl.BlockSpec(memory_space=pltpu.VMEM)] * 4,
        out_shape=[
            jax.ShapeDtypeStruct((ne, cap), I32),
            jax.ShapeDtypeStruct((ne, cap), F32),
            jax.ShapeDtypeStruct((ne, cap), I32),
            jax.ShapeDtypeStruct((nchunk, LANES), F32),
        ],
        scratch_shapes=[pltpu.VMEM((ne, nchunk, LANES), F32), pltpu.VMEM((ne, nchunk, LANES), F32)],
        compiler_params=pltpu.CompilerParams(vmem_limit_bytes=VMEM_LIMIT),
        name="route",
    )(aff3)


def _gather_body(idx_ref, src_hbm, o_ref, sem, *, rows):
    def row_copy(i):
        return pltpu.make_async_copy(src_hbm.at[pl.ds(idx_ref[0, 0, i], 1), :], o_ref.at[pl.ds(i, 1), :], sem)

    def start(i, c):
        row_copy(i).start()
        return c

    def wait(i, c):
        row_copy(i).wait()
        return c

    lax.fori_loop(0, rows, start, 0, unroll=8)
    lax.fori_loop(0, rows, wait, 0, unroll=8)


def _gather_rows(src, idx_flat):
    nrows = idx_flat.shape[0]
    d = src.shape[1]
    rows = ROW_BLK
    nb = nrows // rows
    return pl.pallas_call(
        functools.partial(_gather_body, rows=rows),
        grid=(nb,),
        in_specs=[
            pl.BlockSpec((1, 1, rows), lambda i: (i, 0, 0), memory_space=pltpu.SMEM),
            pl.BlockSpec(memory_space=pl.ANY),
        ],
        out_specs=pl.BlockSpec((rows, d), lambda i: (i, 0)),
        out_shape=jax.ShapeDtypeStruct((nrows, d), src.dtype),
        scratch_shapes=[pltpu.SemaphoreType.DMA(())],
        compiler_params=_cparams(("arbitrary",)),
        name="gather_rows",
    )(idx_flat.reshape(nb, 1, rows), src)


def _scatter_body(pos_ref, y_ref, dst_hbm, sem, *, rows):
    def row_copy(i):
        return pltpu.make_async_copy(y_ref.at[pl.ds(i, 1), :], dst_hbm.at[pl.ds(pos_ref[0, 0, i], 1), :], sem)

    def start(i, c):
        row_copy(i).start()
        return c

    def wait(i, c):
        row_copy(i).wait()
        return c

    lax.fori_loop(0, rows, start, 0, unroll=8)
    lax.fori_loop(0, rows, wait, 0, unroll=8)


def _scatter_rows(y, pos_flat):
    nrows, d = y.shape
    rows = ROW_BLK
    nb = nrows // rows
    return pl.pallas_call(
        functools.partial(_scatter_body, rows=rows),
        grid=(nb,),
        in_specs=[
            pl.BlockSpec((1, 1, rows), lambda i: (i, 0, 0), memory_space=pltpu.SMEM),
            pl.BlockSpec((rows, d), lambda i: (i, 0)),
        ],
        out_specs=pl.BlockSpec(memory_space=pl.ANY),
        out_shape=jax.ShapeDtypeStruct((nrows, d), y.dtype),
        scratch_shapes=[pltpu.SemaphoreType.DMA(())],
        compiler_params=_cparams(("arbitrary",)),
        name="scatter_rows",
    )(pos_flat.reshape(nb, 1, rows), y)


def _ffn_body(x_ref, wg_ref, wu_ref, wd_ref, gate_ref, tok_ref, o_ref, wg_s, wu_s, wd_s):
    @pl.when(pl.program_id(1) == 0)
    def _():
        wg_s[...] = wg_ref[0].astype(BF16)
        wu_s[...] = wu_ref[0].astype(BF16)
        wd_s[...] = wd_ref[0].astype(BF16)

    x = x_ref[...].astype(BF16)
    g = jnp.dot(x, wg_s[...], preferred_element_type=F32)
    u = jnp.dot(x, wu_s[...], preferred_element_type=F32)
    hid = (g * (1.0 / (1.0 + jnp.exp(-g))) * u).astype(BF16)
    y = jnp.dot(hid, wd_s[...], preferred_element_type=F32) * gate_ref[...]
    d = y.shape[1]
    o_ref[:, :d] = y
    o_ref[:, d:] = jnp.broadcast_to(tok_ref[...].astype(F32), (y.shape[0], AUG))


def _expert_ffn(xe, w_gate, w_up, w_down, gate_col, tok_col, cap):
    nrows, d = xe.shape
    ne, _, f = w_gate.shape
    tc = FFN_TC
    nc = cap // tc
    return pl.pallas_call(
        _ffn_body,
        grid=(ne, nc),
        in_specs=[
            pl.BlockSpec((tc, d), lambda e, j: (e * nc + j, 0)),
            pl.BlockSpec((1, d, f), lambda e, j: (e, 0, 0)),
            pl.BlockSpec((1, d, f), lambda e, j: (e, 0, 0)),
            pl.BlockSpec((1, f, d), lambda e, j: (e, 0, 0)),
            pl.BlockSpec((tc, 1), lambda e, j: (e * nc + j, 0)),
            pl.BlockSpec((tc, 1), lambda e, j: (e * nc + j, 0)),
        ],
        out_specs=pl.BlockSpec((tc, d + AUG), lambda e, j: (e * nc + j, 0)),
        out_shape=jax.ShapeDtypeStruct((nrows, d + AUG), F32),
        scratch_shapes=[pltpu.VMEM((d, f), BF16), pltpu.VMEM((d, f), BF16), pltpu.VMEM((f, d), BF16)],
        compiler_params=_cparams(("arbitrary", "arbitrary")),
        name="expert_ffn",
    )(xe, w_gate, w_up, w_down, gate_col, tok_col)


def _combine_body(tile_ref, chunk_ref, valid_ref, y_ref, x_ref, o_ref):
    i = pl.program_id(0)
    t = tile_ref[i]
    prev = tile_ref[jnp.maximum(i - 1, 0)]

    @pl.when((i == 0) | (prev != t))
    def _():
        o_ref[...] = x_ref[...]

    @pl.when(valid_ref[i] != 0)
    def _():
        d = o_ref.shape[1]
        tt = o_ref.shape[0]
        y = y_ref[:, :d]
        tokrel = y_ref[:, d:] - (t * tt).astype(F32)
        tok_w = jnp.concatenate([tokrel] * (tt // AUG), axis=1)
        lane = lax.broadcasted_iota(I32, tok_w.shape, 1).astype(F32)
        pt = (tok_w == lane).astype(BF16)
        y_hi = y.astype(BF16)
        y_lo = (y - y_hi.astype(F32)).astype(BF16)
        dn = (((0,), (0,)), ((), ()))
        o_ref[...] += (lax.dot_general(pt, y_hi, dn, preferred_element_type=F32)
                       + lax.dot_general(pt, y_lo, dn, preferred_element_type=F32))


def _combine(y_sorted, x1, tile_id, chunk_id, valid):
    t, d = x1.shape
    n_items = tile_id.shape[0]
    grid_spec = pltpu.PrefetchScalarGridSpec(
        num_scalar_prefetch=3,
        grid=(n_items,),
        in_specs=[
            pl.BlockSpec((SEG_CH, d + AUG), lambda i, tl, ch, va: (ch[i], 0)),
            pl.BlockSpec((SEG_TT, d), lambda i, tl, ch, va: (tl[i], 0)),
        ],
        out_specs=pl.BlockSpec((SEG_TT, d), lambda i, tl, ch, va: (tl[i], 0)),
    )
    return pl.pallas_call(
        _combine_body,
        grid_spec=grid_spec,
        out_shape=jax.ShapeDtypeStruct((t, d), F32),
        compiler_params=_cparams(("arbitrary",)),
        name="combine",
    )(tile_id, chunk_id, valid, y_sorted, x1)


def _combine_plan(base_flat, nrows):
    t = base_flat.shape[0]
    ntiles = t // SEG_TT
    nchunks = nrows // SEG_CH
    n_items = ntiles + nchunks
    b = jnp.concatenate([base_flat[::SEG_TT].astype(I32), jnp.full((1,), nrows, I32)])
    lo = jnp.minimum(b[:-1] // SEG_CH, nchunks - 1)
    hi = jnp.maximum((b[1:] - 1) // SEG_CH, lo)
    cnt = hi - lo + 1
    ends = jnp.cumsum(cnt)
    starts = ends - cnt
    item = jnp.arange(n_items, dtype=I32)
    tile = jnp.minimum(jnp.searchsorted(ends, item, side="right").astype(I32), ntiles - 1)
    chunk = jnp.minimum(lo[tile] + item - starts[tile], nchunks - 1)
    valid = (item < ends[-1]).astype(I32)
    return tile, chunk, valid


def _moe(x1, h, aff_t, w_gate, w_up, w_down):
    t = x1.shape[0]
    cap = EC_CAPACITY_FACTOR * t // N_EXPERTS
    idx, gate, pos, base = _route(aff_t, cap)
    idx_flat = idx.reshape(-1)
    xe = _gather_rows(h, idx_flat)
    ye = _expert_ffn(xe, w_gate, w_up, w_down, gate.reshape(-1, 1), idx_flat.reshape(-1, 1), cap)
    ys = _scatter_rows(ye, pos.reshape(-1))
    tile, chunk, valid = _combine_plan(base.reshape(-1), ye.shape[0])
    return _combine(ys, x1, tile, chunk, valid)


def _block_diag_ones():
    i = np.arange(CHUNK) // HEAD_DIM
    return jnp.asarray((i[:, None] == i[None, :]).astype(np.float32), dtype=BF16)


def _router_split(w_router):
    w = jnp.pad(w_router.astype(F32), ((0, 0), (0, LANES - N_EXPERTS)))
    hi = w.astype(BF16)
    lo = (w - hi.astype(F32)).astype(BF16)
    return hi, lo


def _layer_a(x, rel_bias, norm_mix, w_in, q_gain, k_gain, w_out, norm_ffn, router):
    bsz, seq, d = x.shape
    t = bsz * seq
    x2d = x.reshape(t, d)
    qk_gain = jnp.concatenate([jnp.tile(q_gain.astype(F32), N_HEADS) * (HEAD_DIM ** -0.5),
                               jnp.tile(k_gain.astype(F32), N_HEADS),
                               jnp.ones((MIX_WIDTH,), F32)])[None]
    qkv = _project(x2d, norm_mix[None].astype(F32), w_in.astype(BF16), _block_diag_ones(), qk_gain,
                   n_norm_chunks=2 * MIX_WIDTH // CHUNK)
    o_list, lse_list = [], []
    for g, (window, dil) in enumerate(DILATED_GROUPS):
        half = window // (2 * dil)
        n = seq // dil
        heads = list(range(g * HEADS_PER_CHUNK, (g + 1) * HEADS_PER_CHUNK))
        bias = _bias_table(rel_bias, heads, half, min(3 * half, n), dil)
        o, lse = _banded_attention(
            qkv.reshape(bsz, n, dil * 3 * MIX_WIDTH), bias, None, batch=bsz, n=n, r=dil, qb=half, gq=1,
            q_col=g, k_col=MIX_WIDTH // CHUNK + g, v_col=2 * MIX_WIDTH // CHUNK + g,
            width_in=3 * MIX_WIDTH, width_out=CHUNK, out_dtype=F32, want_lse=True)
        o_list.append(o.reshape(t, CHUNK))
        lse_list.append(lse.reshape(t, CHUNK))
    r_hi, r_lo = _router_split(router)
    return _out_project(o_list, lse_list, x2d, w_out.astype(BF16), norm_ffn[None].astype(F32), r_hi, r_lo)


def _gqa_perm():
    g, kv, dd = np.meshgrid(np.arange(Q_PER_KV), np.arange(N_KV_HEADS), np.arange(HEAD_DIM), indexing="ij")
    return (kv * Q_PER_KV * HEAD_DIM + g * HEAD_DIM + dd).reshape(-1)


def _layer_b(x, rel_bias, norm_mix, w_in, q_gain, k_gain, sink, w_out, norm_ffn, router):
    bsz, seq, d = x.shape
    t = bsz * seq
    x2d = x.reshape(t, d)
    perm = _gqa_perm()
    w_in_p = jnp.concatenate([w_in[:, :MIX_WIDTH][:, perm], w_in[:, MIX_WIDTH:]], axis=1).astype(BF16)
    w_out_p = w_out[perm, :].astype(BF16)
    qk_gain = jnp.concatenate([jnp.tile(q_gain.astype(F32), N_HEADS) * (HEAD_DIM ** -0.5),
                               jnp.tile(k_gain.astype(F32), N_KV_HEADS),
                               jnp.ones((KV_WIDTH,), F32)])[None]
    width = MIX_WIDTH + 2 * KV_WIDTH
    qkv = _project(x2d, norm_mix[None].astype(F32), w_in_p, _block_diag_ones(), qk_gain,
                   n_norm_chunks=(MIX_WIDTH + KV_WIDTH) // CHUNK)
    half = LOCAL_HALF_WINDOW
    heads = [kv * Q_PER_KV + g for g in range(Q_PER_KV) for kv in range(N_KV_HEADS)]
    bias = _bias_table(rel_bias, heads, half, min(3 * half, seq), 1)
    sink_tab = jnp.repeat(sink.astype(F32)[np.asarray(heads)].reshape(Q_PER_KV, N_KV_HEADS), half, axis=1)[..., None]
    (o,) = _banded_attention(
        qkv.reshape(bsz, seq, width), bias, sink_tab, batch=bsz, n=seq, r=1, qb=half, gq=Q_PER_KV,
        q_col=0, k_col=MIX_WIDTH // CHUNK, v_col=MIX_WIDTH // CHUNK + 1,
        width_in=width, width_out=MIX_WIDTH, out_dtype=BF16, want_lse=False)
    r_hi, r_lo = _router_split(router)
    return _out_project([o.reshape(t, MIX_WIDTH)], [], x2d, w_out_p, norm_ffn[None].astype(F32), r_hi, r_lo)


def _trunk(x, rel_bias, norm_mix, norm_ffn, a_w_in, a_q_gain, a_k_gain, a_w_out, b_w_in, b_q_gain, b_k_gain,
           b_sink, b_w_out, moe_router, moe_w_gate, moe_w_up, moe_w_down):
    shape = x.shape
    for i in range(DEPTH):
        j = i // 2
        if i % 2 == 0:
            x1, h, aff_t = _layer_a(x, rel_bias, norm_mix[i], a_w_in[j], a_q_gain[j], a_k_gain[j], a_w_out[j],
                                    norm_ffn[i], moe_router[i])
        else:
            x1, h, aff_t = _layer_b(x, rel_bias, norm_mix[i], b_w_in[j], b_q_gain[j], b_k_gain[j], b_sink[j],
                                    b_w_out[j], norm_ffn[i], moe_router[i])
        x = _moe(x1, h, aff_t, moe_w_gate[i], moe_w_up[i], moe_w_down[i]).reshape(shape)
    return x


def kernel(x_prompt, x_sample, rel_bias, norm_mix, norm_ffn, a_w_in, a_q_gain, a_k_gain, a_w_out, b_w_in, b_q_gain, b_k_gain, b_sink, b_w_out, moe_router, moe_w_gate, moe_w_up, moe_w_down):
    weights = (rel_bias, norm_mix, norm_ffn, a_w_in, a_q_gain, a_k_gain, a_w_out, b_w_in, b_q_gain, b_k_gain,
               b_sink, b_w_out, moe_router, moe_w_gate, moe_w_up, moe_w_down)
    return _trunk(x_prompt, *weights), _trunk(x_sample, *weights)
```

```python
import functools
import math

import numpy as np
import jax
import jax.numpy as jnp
from jax import lax
from jax.experimental import pallas as pl
from jax.experimental.pallas import tpu as pltpu

F32 = jnp.float32
BF16 = jnp.bfloat16
I32 = jnp.int32

D_MODEL = 1024
N_HEADS = 12
HEAD_DIM = 64
MIX_WIDTH = N_HEADS * HEAD_DIM
DILATED_GROUPS = ((128, 1), (512, 4), (2048, 16))
N_KV_HEADS = 4
Q_PER_KV = N_HEADS // N_KV_HEADS
KV_WIDTH = N_KV_HEADS * HEAD_DIM
LOCAL_HALF_WINDOW = 128
N_BUCKETS = 32
REL_MAX_DISTANCE = 1024
N_EXPERTS = 16
EC_CAPACITY_FACTOR = 2
D_EXPERT = 1024
DEPTH = 2
RMS_EPS = 1e-6
NEG_INF = -1e30

LANES = 128
HEADS_PER_CHUNK = 4
CHUNK = HEADS_PER_CHUNK * HEAD_DIM
VMEM_LIMIT = 56 * 1024 * 1024

PROJ_TM = 512
ATTN_TQ = 512
ATTN_UNROLL_A = 2
ATTN_UNROLL_B = 1
FFN_TC = 256
ROW_BLK = 256
SEG_TT = 256
SEG_CH = 256
AUG = LANES


def _cparams(sem):
    return pltpu.CompilerParams(dimension_semantics=sem, vmem_limit_bytes=VMEM_LIMIT)


def _proj_body(x_ref, g_ref, w_ref, bd_ref, qk_ref, *rest, n_norm_chunks, groups):
    o_refs = rest[:len(groups)]
    fold_scr = rest[len(groups)] if len(rest) > len(groups) else None
    x = x_ref[0]
    tm = x.shape[0]
    ms = jnp.mean(x * x, axis=-1, keepdims=True)
    h = (x * lax.rsqrt(ms + RMS_EPS) * g_ref[...]).astype(BF16)
    acc = jnp.dot(h, w_ref[...], preferred_element_type=F32)
    bd = bd_ref[...]
    for og, (r, chunks) in enumerate(groups):
        for k, c in enumerate(chunks):
            sl = slice(c * CHUNK, (c + 1) * CHUNK)
            a = acc[:, sl]
            if c < n_norm_chunks:
                a2 = a * a
                hi = a2.astype(BF16)
                lo = (a2 - hi.astype(F32)).astype(BF16)
                ss = jnp.dot(hi, bd, preferred_element_type=F32) + jnp.dot(lo, bd, preferred_element_type=F32)
                a = a * lax.rsqrt(ss * (1.0 / HEAD_DIM) + RMS_EPS) * qk_ref[:, sl]
            if r == 1:
                o_refs[og][0, 0, :, k * CHUNK:(k + 1) * CHUNK] = a.astype(BF16)
            else:
                for s in range(CHUNK // LANES):
                    fold_scr[s] = a[:, s * LANES:(s + 1) * LANES]
                for p in range(r):
                    for s in range(CHUNK // LANES):
                        lo_lane = k * CHUNK + s * LANES
                        o_refs[og][0, p, :, lo_lane:lo_lane + LANES] = (
                            fold_scr[s, pl.ds(p, tm // r, stride=r), :].astype(BF16))


def _project(x3d, gain, w_bf, bd, qk_gain, n_norm_chunks, groups):
    bsz, seq, d = x3d.shape
    n = w_bf.shape[1]
    tm = PROJ_TM
    scratch = [pltpu.VMEM((CHUNK // LANES, tm, LANES), F32)] if any(r > 1 for r, _ in groups) else []
    return pl.pallas_call(
        functools.partial(_proj_body, n_norm_chunks=n_norm_chunks, groups=groups),
        grid=(bsz, seq // tm),
        in_specs=[
            pl.BlockSpec((1, tm, d), lambda b, i: (b, i, 0)),
            pl.BlockSpec((1, d), lambda b, i: (0, 0)),
            pl.BlockSpec((d, n), lambda b, i: (0, 0)),
            pl.BlockSpec((CHUNK, CHUNK), lambda b, i: (0, 0)),
            pl.BlockSpec((1, n), lambda b, i: (0, 0)),
        ],
        out_specs=[pl.BlockSpec((1, r, tm // r, CHUNK * len(ch)), lambda b, i: (b, 0, i, 0)) for r, ch in groups],
        out_shape=[jax.ShapeDtypeStruct((bsz, r, seq // r, CHUNK * len(ch)), BF16) for r, ch in groups],
        scratch_shapes=scratch,
        compiler_params=_cparams(("parallel", "parallel")),
        name="proj",
    )(x3d, gain, w_bf, bd, qk_gain)


def _attn_body(*refs, qb, wk, n, gq, nblk, has_sink, want_lse, unroll):
    q_ref, k_ref, v_ref, bias_ref = refs[:4]
    pos = 4
    sink_ref = None
    if has_sink:
        sink_ref = refs[pos]
        pos += 1
    o_ref = refs[pos]
    lse_ref = refs[pos + 1] if want_lse else None

    jt = pl.program_id(2)
    lane_head = lax.broadcasted_iota(I32, (1, CHUNK), 1) // HEAD_DIM
    masks = [lane_head == h for h in range(HEADS_PER_CHUNK)]

    def block(blk, carry):
        i = jt * nblk + blk
        ws = jnp.clip((i - 1) * qb, 0, n - wk)
        var = i - ws // qb
        ws = pl.multiple_of(ws, qb)
        r0 = pl.multiple_of(blk * qb, qb)
        kb = k_ref[0, 0, pl.ds(ws, wk), :]
        vb = v_ref[0, 0, pl.ds(ws, wk), :]
        for g in range(gq):
            cs = slice(g * CHUNK, (g + 1) * CHUNK)
            qg = q_ref[0, 0, pl.ds(r0, qb), cs]
            zero = jnp.zeros_like(qg)
            q4 = jnp.concatenate([jnp.where(m, qg, zero) for m in masks], axis=0)
            s = lax.dot_general(q4, kb, (((1,), (1,)), ((), ())), preferred_element_type=F32)
            s = s + bias_ref[var, g]
            m = jnp.max(s, axis=-1, keepdims=True)
            if has_sink:
                snk = sink_ref[g]
                m = jnp.maximum(m, snk)
            e = jnp.exp(s - m)
            l = jnp.sum(e, axis=-1, keepdims=True)
            if has_sink:
                l = l + jnp.exp(snk - m)
            p = (e * (1.0 / l)).astype(BF16)
            pv = jnp.dot(p, vb, preferred_element_type=F32)
            o = jnp.zeros((qb, CHUNK), F32)
            for h in range(HEADS_PER_CHUNK):
                o = jnp.where(masks[h], pv[h * qb:(h + 1) * qb], o)
            o_ref[0, 0, pl.ds(r0, qb), cs] = o.astype(o_ref.dtype)
            if want_lse:
                lse = m + jnp.log(l)
                lo = jnp.zeros((qb, CHUNK), F32)
                for h in range(HEADS_PER_CHUNK):
                    lo = jnp.where(masks[h], lse[h * qb:(h + 1) * qb], lo)
                lse_ref[0, 0, pl.ds(r0, qb), cs] = lo
        return carry

    lax.fori_loop(0, nblk, block, 0, unroll=unroll)


def _t5_bucket(rel):
    half = N_BUCKETS // 2
    max_exact = half // 2
    nn = -rel
    base = np.where(nn < 0, half, 0)
    nn = np.abs(nn)
    large = max_exact + (np.log(np.maximum(nn, 1) / max_exact)
                         / math.log(REL_MAX_DISTANCE / max_exact) * (half - max_exact)).astype(np.int32)
    large = np.minimum(large, half - 1)
    return (base + np.where(nn < max_exact, nn, large)).astype(np.int32)


def _bias_body(heads_ref, bucket_ref, rb_ref, o_ref):
    b = bucket_ref[0]
    head = heads_ref[pl.program_id(1)]

    def step(k, acc):
        return jnp.where(b == k, rb_ref[k, head], acc)

    o_ref[0, 0] = lax.fori_loop(0, N_BUCKETS, step, jnp.full(b.shape, NEG_INF, F32))


def _bias_table(rel_bias, heads, qb, wk, dil):
    var = np.arange(3)[:, None, None]
    rel = np.arange(wk)[None, None, :] - np.arange(qb)[None, :, None] - var * qb
    bucket = np.where(np.abs(rel) <= qb, _t5_bucket(rel * dil), -1).astype(np.int32)
    nh = len(heads)
    grid_spec = pltpu.PrefetchScalarGridSpec(
        num_scalar_prefetch=1,
        grid=(3, nh),
        in_specs=[
            pl.BlockSpec((1, qb, wk), lambda v, hh, hd: (v, 0, 0)),
            pl.BlockSpec(memory_space=pltpu.SMEM),
        ],
        out_specs=pl.BlockSpec((1, 1, qb, wk), lambda v, hh, hd: (v, hh, 0, 0)),
    )
    tab = pl.pallas_call(
        _bias_body,
        grid_spec=grid_spec,
        out_shape=jax.ShapeDtypeStruct((3, nh, qb, wk), F32),
        compiler_params=_cparams(("arbitrary", "arbitrary")),
        name="bias_table",
    )(jnp.asarray(np.asarray(heads, np.int32)), jnp.asarray(bucket), rel_bias.astype(F32))
    return tab.reshape(3, nh // HEADS_PER_CHUNK, HEADS_PER_CHUNK * qb, wk)


def _banded_attention(q_arr, kv_arr, bias, sink, *, qb, gq, q_col, k_col, v_col, out_dtype, want_lse, unroll):
    batch, r, n, _ = q_arr.shape
    wk = min(3 * qb, n)
    tq = min(ATTN_TQ, n)
    nblk = tq // qb
    qw = gq * CHUNK
    in_specs = [
        pl.BlockSpec((1, 1, tq, qw), lambda b, p, j: (b, p, j, q_col)),
        pl.BlockSpec((1, 1, n, CHUNK), lambda b, p, j: (b, p, 0, k_col)),
        pl.BlockSpec((1, 1, n, CHUNK), lambda b, p, j: (b, p, 0, v_col)),
        pl.BlockSpec(bias.shape, lambda b, p, j: (0, 0, 0, 0)),
    ]
    args = [q_arr, kv_arr, kv_arr, bias]
    if sink is not None:
        in_specs.append(pl.BlockSpec(sink.shape, lambda b, p, j: (0, 0, 0)))
        args.append(sink)
    o_spec = pl.BlockSpec((1, 1, tq, qw), lambda b, p, j: (b, p, j, 0))
    out_specs, out_shape = [o_spec], [jax.ShapeDtypeStruct((batch, r, n, qw), out_dtype)]
    if want_lse:
        out_specs.append(o_spec)
        out_shape.append(jax.ShapeDtypeStruct((batch, r, n, qw), F32))
    return pl.pallas_call(
        functools.partial(_attn_body, qb=qb, wk=wk, n=n, gq=gq, nblk=nblk, has_sink=sink is not None,
                          want_lse=want_lse, unroll=unroll),
        grid=(batch, r, n // tq),
        in_specs=in_specs,
        out_specs=out_specs,
        out_shape=out_shape,
        compiler_params=_cparams(("parallel", "parallel", "arbitrary")),
        name="attn",
    )(*args)


def _unfold(ref, scr, r):
    if r == 1:
        return ref[0, 0]
    rows = ref.shape[2]
    for p in range(r):
        for s in range(CHUNK // LANES):
            scr[s, pl.ds(p, rows, stride=r), :] = ref[0, p, :, s * LANES:(s + 1) * LANES]
    return jnp.concatenate([scr[s] for s in range(CHUNK // LANES)], axis=1)


def _outproj_body(*refs, dils):
    n_mix = len(dils)
    if n_mix:
        o_refs, lse_refs = refs[:n_mix], refs[n_mix:2 * n_mix]
        x_ref, w_ref, g_ref, rh_ref, rl_ref, x1_ref, h_ref, aff_ref = refs[2 * n_mix:2 * n_mix + 8]
        scrs = list(refs[2 * n_mix + 8:])
        os_, ls = [], []
        for g, r in enumerate(dils):
            os_.append(_unfold(o_refs[g], scrs.pop(0) if r > 1 else None, r))
            ls.append(_unfold(lse_refs[g], scrs.pop(0) if r > 1 else None, r))
        mx = functools.reduce(jnp.maximum, ls)
        es = [jnp.exp(l - mx) for l in ls]
        inv = 1.0 / functools.reduce(lambda a, b: a + b, es)
        mixed = jnp.concatenate([(es[g] * inv) * os_[g] for g in range(n_mix)], axis=1).astype(BF16)
    else:
        o_ref, x_ref, w_ref, g_ref, rh_ref, rl_ref, x1_ref, h_ref, aff_ref = refs
        mixed = o_ref[0, 0]
    x1 = x_ref[0] + jnp.dot(mixed, w_ref[...], preferred_element_type=F32)
    x1_ref[0] = x1
    ms = jnp.mean(x1 * x1, axis=-1, keepdims=True)
    h = x1 * lax.rsqrt(ms + RMS_EPS) * g_ref[...]
    h_ref[0] = h
    h_hi = h.astype(BF16)
    h_lo = (h - h_hi.astype(F32)).astype(BF16)
    rh = rh_ref[...]
    logits = (jnp.dot(h_hi, rh, preferred_element_type=F32) + jnp.dot(h_lo, rh, preferred_element_type=F32)
              + jnp.dot(h_hi, rl_ref[...], preferred_element_type=F32))
    lt = logits.T[:N_EXPERTS]
    mx = jnp.max(lt, axis=0, keepdims=True)
    e = jnp.exp(lt - mx)
    aff_ref[...] = e / jnp.sum(e, axis=0, keepdims=True)


def _out_project(o_list, lse_list, x3d, w_bf, gain, r_hi, r_lo):
    bsz, seq, d = x3d.shape
    tm = PROJ_TM
    nt = seq // tm
    dils = tuple(a.shape[1] for a in lse_list)
    args = list(o_list) + list(lse_list)
    in_specs = [pl.BlockSpec((1, a.shape[1], tm // a.shape[1], a.shape[3]), lambda b, i: (b, 0, i, 0)) for a in args]
    in_specs += [
        pl.BlockSpec((1, tm, d), lambda b, i: (b, i, 0)),
        pl.BlockSpec((w_bf.shape[0], d), lambda b, i: (0, 0)),
        pl.BlockSpec((1, d), lambda b, i: (0, 0)),
        pl.BlockSpec((d, LANES), lambda b, i: (0, 0)),
        pl.BlockSpec((d, LANES), lambda b, i: (0, 0)),
    ]
    args += [x3d, w_bf, gain, r_hi, r_lo]
    n_scr = 2 * sum(1 for r in dils if r > 1)
    return pl.pallas_call(
        functools.partial(_outproj_body, dils=dils),
        grid=(bsz, nt),
        in_specs=in_specs,
        out_specs=[
            pl.BlockSpec((1, tm, d), lambda b, i: (b, i, 0)),
            pl.BlockSpec((1, tm, d), lambda b, i: (b, i, 0)),
            pl.BlockSpec((N_EXPERTS, tm), lambda b, i: (0, b * nt + i)),
        ],
        out_shape=[
            jax.ShapeDtypeStruct((bsz, seq, d), F32),
            jax.ShapeDtypeStruct((bsz, seq, d), F32),
            jax.ShapeDtypeStruct((N_EXPERTS, bsz * seq), F32),
        ],
        scratch_shapes=[pltpu.VMEM((CHUNK // LANES, tm, LANES), F32)] * n_scr,
        compiler_params=_cparams(("parallel", "parallel")),
        name="outproj",
    )(*args)


def _count(m):
    return jnp.sum(jnp.sum(m.astype(F32), axis=0, keepdims=True), axis=1, keepdims=True)


def _split3(x):
    hi = x.astype(BF16)
    r1 = x - hi.astype(F32)
    mid = r1.astype(BF16)
    lo = (r1 - mid.astype(F32)).astype(BF16)
    return hi, mid, lo


def _route_body(aff_ref, idx_ref, gate_ref, pos_ref, base_ref, mask_scr, pos_scr, *, cap, nchunk):
    ne = N_EXPERTS
    cpos = LANES
    sub_i = lax.broadcasted_iota(I32, (nchunk, cpos), 0)
    lane_i = lax.broadcasted_iota(I32, (nchunk, cpos), 1)
    tok = sub_i * cpos + lane_i
    capf = jnp.float32(cap)

    def bits(e):
        return pltpu.bitcast(aff_ref[e], I32)

    def val_step(it, taus):
        bit = jnp.left_shift(jnp.int32(1), 30 - it)
        out = []
        for e in range(ne):
            cand = taus[e] | bit
            cnt = _count(bits(e) >= cand)
            out.append(jnp.where(cnt >= capf, cand, taus[e]))
        return tuple(out)

    taus = lax.fori_loop(0, 31, val_step, tuple(jnp.zeros((1, 1), I32) for _ in range(ne)))
    need = [capf - _count(bits(e) > taus[e]) for e in range(ne)]

    tbits = int(math.log2(nchunk * cpos))

    def tie_step(it, vs):
        bit = jnp.left_shift(jnp.int32(1), tbits - 1 - it)
        out = []
        for e in range(ne):
            cand = vs[e] | bit
            cnt = _count((bits(e) == taus[e]) & (tok < cand))
            out.append(jnp.where(cnt < need[e], cand, vs[e]))
        return tuple(out)

    vs = lax.fori_loop(0, tbits, tie_step, tuple(jnp.zeros((1, 1), I32) for _ in range(ne)))

    ksum = jnp.zeros((nchunk, cpos), F32)
    for e in range(ne):
        b = bits(e)
        m = ((b > taus[e]) | ((b == taus[e]) & (tok <= vs[e]))).astype(F32)
        mask_scr[e] = m
        pos_scr[e] = ksum
        ksum = ksum + m
    tri_u = (lax.broadcasted_iota(I32, (cpos, cpos), 0)
             <= lax.broadcasted_iota(I32, (cpos, cpos), 1)).astype(BF16)
    cs_i = lax.broadcasted_iota(I32, (nchunk, nchunk), 0)
    cl_i = lax.broadcasted_iota(I32, (nchunk, nchunk), 1)
    tri_ls = (cl_i < cs_i).astype(BF16)
    tri_li = (cl_i <= cs_i).astype(BF16)
    kin = jnp.dot(ksum.astype(BF16), tri_u, preferred_element_type=F32)
    ktot = jnp.broadcast_to(kin[:, cpos - 1:cpos], (nchunk, cpos))
    khi = jnp.floor(ktot * (1.0 / 256.0))
    klo = ktot - 256.0 * khi
    kstart = (256.0 * jnp.dot(tri_ls, khi.astype(BF16), preferred_element_type=F32)
              + jnp.dot(tri_ls, klo.astype(BF16), preferred_element_type=F32))
    base = kstart + kin - ksum
    base_ref[...] = base

    s_row = lax.broadcasted_iota(I32, (1, cap), 1).astype(F32)
    c_col = lax.broadcasted_iota(I32, (nchunk, cap), 0).astype(F32)
    j_col = lax.broadcasted_iota(I32, (cpos, cap), 0).astype(F32)
    reps = cap // cpos
    for e in range(ne):
        m = mask_scr[e]
        posv = pos_scr[e] + base
        lin = jnp.dot(m.astype(BF16), tri_u, preferred_element_type=F32)
        tot = jnp.broadcast_to(lin[:, cpos - 1:cpos], (nchunk, cpos))
        cend = jnp.dot(tri_li, tot.astype(BF16), preferred_element_type=F32)
        cstart = cend - tot
        cend_w = jnp.concatenate([cend] * reps, axis=1)
        cstart_w = jnp.concatenate([cstart] * reps, axis=1)
        cidx = jnp.sum((cend_w <= s_row).astype(F32), axis=0, keepdims=True)
        oh = c_col == cidx
        ohb = oh.astype(BF16)
        cst = jnp.sum(jnp.where(oh, cstart_w, 0.0), axis=0, keepdims=True)
        target = s_row - cst
        g = jnp.dot(lin.T.astype(BF16), ohb, preferred_element_type=F32)
        jidx = jnp.sum((g <= target).astype(F32), axis=0, keepdims=True)
        ohj = j_col == jidx
        idx_ref[e:e + 1, :] = (cidx * float(cpos) + jidx).astype(I32)
        a_hi, a_mid, a_lo = _split3(aff_ref[e].T)
        ga = (jnp.dot(a_hi, ohb, preferred_element_type=F32) + jnp.dot(a_mid, ohb, preferred_element_type=F32)
              + jnp.dot(a_lo, ohb, preferred_element_type=F32))
        gate_ref[e:e + 1, :] = jnp.sum(jnp.where(ohj, ga, 0.0), axis=0, keepdims=True)
        pt = posv.T
        p_hi = jnp.floor(pt * (1.0 / 256.0))
        p_lo = pt - 256.0 * p_hi
        gp = (256.0 * jnp.dot(p_hi.astype(BF16), ohb, preferred_element_type=F32)
              + jnp.dot(p_lo.astype(BF16), ohb, preferred_element_type=F32))
        pos_ref[e:e + 1, :] = jnp.sum(jnp.where(ohj, gp, 0.0), axis=0, keepdims=True).astype(I32)


def _route(aff_t, cap):
    ne, t = aff_t.shape
    nchunk = t // LANES
    aff3 = aff_t.reshape(ne, nchunk, LANES)
    return pl.pallas_call(
        functools.partial(_route_body, cap=cap, nchunk=nchunk),
        in_specs=[pl.BlockSpec(memory_space=pltpu.VMEM)],
        out_specs=[pl.BlockSpec(memory_space=pltpu.VMEM)] * 4,
        out_shape=[
            jax.ShapeDtypeStruct((ne, cap), I32),
            jax.ShapeDtypeStruct((ne, cap), F32),
            jax.ShapeDtypeStruct((ne, cap), I32),
            jax.ShapeDtypeStruct((nchunk, LANES), F32),
        ],
        scratch_shapes=[pltpu.VMEM((ne, nchunk, LANES), F32), pltpu.VMEM((ne, nchunk, LANES), F32)],
        compiler_params=pltpu.CompilerParams(vmem_limit_bytes=VMEM_LIMIT),
        name="route",
    )(aff3)


def _gather_body(idx_ref, src_hbm, o_ref, sem, *, rows):
    def row_copy(i):
        return pltpu.make_async_copy(src_hbm.at[pl.ds(idx_ref[0, 0, i], 1), :], o_ref.at[pl.ds(i, 1), :], sem)

    def start(i, c):
        row_copy(i).start()
        return c

    def wait(i, c):
        row_copy(i).wait()
        return c

    lax.fori_loop(0, rows, start, 0, unroll=8)
    lax.fori_loop(0, rows, wait, 0, unroll=8)


def _gather_rows(src, idx_flat):
    nrows = idx_flat.shape[0]
    d = src.shape[1]
    rows = ROW_BLK
    nb = nrows // rows
    return pl.pallas_call(
        functools.partial(_gather_body, rows=rows),
        grid=(nb,),
        in_specs=[
            pl.BlockSpec((1, 1, rows), lambda i: (i, 0, 0), memory_space=pltpu.SMEM),
            pl.BlockSpec(memory_space=pl.ANY),
        ],
        out_specs=pl.BlockSpec((rows, d), lambda i: (i, 0)),
        out_shape=jax.ShapeDtypeStruct((nrows, d), src.dtype),
        scratch_shapes=[pltpu.SemaphoreType.DMA(())],
        compiler_params=_cparams(("arbitrary",)),
        name="gather_rows",
    )(idx_flat.reshape(nb, 1, rows), src)


def _scatter_body(pos_ref, y_ref, dst_hbm, sem, *, rows):
    def row_copy(i):
        return pltpu.make_async_copy(y_ref.at[pl.ds(i, 1), :], dst_hbm.at[pl.ds(pos_ref[0, 0, i], 1), :], sem)

    def start(i, c):
        row_copy(i).start()
        return c

    def wait(i, c):
        row_copy(i).wait()
        return c

    lax.fori_loop(0, rows, start, 0, unroll=8)
    lax.fori_loop(0, rows, wait, 0, unroll=8)


def _scatter_rows(y, pos_flat):
    nrows, d = y.shape
    rows = ROW_BLK
    nb = nrows // rows
    return pl.pallas_call(
        functools.partial(_scatter_body, rows=rows),
        grid=(nb,),
        in_specs=[
            pl.BlockSpec((1, 1, rows), lambda i: (i, 0, 0), memory_space=pltpu.SMEM),
            pl.BlockSpec((rows, d), lambda i: (i, 0)),
        ],
        out_specs=pl.BlockSpec(memory_space=pl.ANY),
        out_shape=jax.ShapeDtypeStruct((nrows, d), y.dtype),
        scratch_shapes=[pltpu.SemaphoreType.DMA(())],
        compiler_params=_cparams(("arbitrary",)),
        name="scatter_rows",
    )(pos_flat.reshape(nb, 1, rows), y)


def _ffn_body(x_ref, wg_ref, wu_ref, wd_ref, gate_ref, tok_ref, o_ref, wg_s, wu_s, wd_s):
    @pl.when(pl.program_id(1) == 0)
    def _():
        wg_s[...] = wg_ref[0, 0].astype(BF16)
        wu_s[...] = wu_ref[0, 0].astype(BF16)
        wd_s[...] = wd_ref[0, 0].astype(BF16)

    x = x_ref[...].astype(BF16)
    g = jnp.dot(x, wg_s[...], preferred_element_type=F32)
    u = jnp.dot(x, wu_s[...], preferred_element_type=F32)
    hid = (g * (1.0 / (1.0 + jnp.exp(-g))) * u).astype(BF16)
    y = jnp.dot(hid, wd_s[...], preferred_element_type=F32) * gate_ref[...]
    d = y.shape[1]
    o_ref[:, :d] = y
    o_ref[:, d:] = jnp.broadcast_to(tok_ref[...].astype(F32), (y.shape[0], AUG))


def _expert_ffn(xe, w_gate, w_up, w_down, layer, gate_col, tok_col, cap):
    nrows, d = xe.shape
    _, ne, _, f = w_gate.shape
    tc = FFN_TC
    nc = cap // tc
    return pl.pallas_call(
        _ffn_body,
        grid=(ne, nc),
        in_specs=[
            pl.BlockSpec((tc, d), lambda e, j: (e * nc + j, 0)),
            pl.BlockSpec((1, 1, d, f), lambda e, j: (layer, e, 0, 0)),
            pl.BlockSpec((1, 1, d, f), lambda e, j: (layer, e, 0, 0)),
            pl.BlockSpec((1, 1, f, d), lambda e, j: (layer, e, 0, 0)),
            pl.BlockSpec((tc, 1), lambda e, j: (e * nc + j, 0)),
            pl.BlockSpec((tc, 1), lambda e, j: (e * nc + j, 0)),
        ],
        out_specs=pl.BlockSpec((tc, d + AUG), lambda e, j: (e * nc + j, 0)),
        out_shape=jax.ShapeDtypeStruct((nrows, d + AUG), F32),
        scratch_shapes=[pltpu.VMEM((d, f), BF16), pltpu.VMEM((d, f), BF16), pltpu.VMEM((f, d), BF16)],
        compiler_params=_cparams(("arbitrary", "arbitrary")),
        name="expert_ffn",
    )(xe, w_gate, w_up, w_down, gate_col, tok_col)


def _combine_body(tile_ref, chunk_ref, valid_ref, y_ref, x_ref, o_ref):
    i = pl.program_id(0)
    t = tile_ref[i]
    prev = tile_ref[jnp.maximum(i - 1, 0)]

    @pl.when((i == 0) | (prev != t))
    def _():
        o_ref[...] = x_ref[...]

    @pl.when(valid_ref[i] != 0)
    def _():
        d = o_ref.shape[1]
        tt = o_ref.shape[0]
        y = y_ref[:, :d]
        tokrel = y_ref[:, d:] - (t * tt).astype(F32)
        tok_w = jnp.concatenate([tokrel] * (tt // AUG), axis=1)
        lane = lax.broadcasted_iota(I32, tok_w.shape, 1).astype(F32)
        pt = (tok_w == lane).astype(BF16)
        y_hi = y.astype(BF16)
        y_lo = (y - y_hi.astype(F32)).astype(BF16)
        dn = (((0,), (0,)), ((), ()))
        o_ref[...] += (lax.dot_general(pt, y_hi, dn, preferred_element_type=F32)
                       + lax.dot_general(pt, y_lo, dn, preferred_element_type=F32))


def _combine(y_sorted, x1, tile_id, chunk_id, valid):
    t, d = x1.shape
    n_items = tile_id.shape[0]
    grid_spec = pltpu.PrefetchScalarGridSpec(
        num_scalar_prefetch=3,
        grid=(n_items,),
        in_specs=[
            pl.BlockSpec((SEG_CH, d + AUG), lambda i, tl, ch, va: (ch[i], 0)),
            pl.BlockSpec((SEG_TT, d), lambda i, tl, ch, va: (tl[i], 0)),
        ],
        out_specs=pl.BlockSpec((SEG_TT, d), lambda i, tl, ch, va: (tl[i], 0)),
    )
    return pl.pallas_call(
        _combine_body,
        grid_spec=grid_spec,
        out_shape=jax.ShapeDtypeStruct((t, d), F32),
        compiler_params=_cparams(("arbitrary",)),
        name="combine",
    )(tile_id, chunk_id, valid, y_sorted, x1)


def _combine_plan(base_flat, nrows):
    t = base_flat.shape[0]
    ntiles = t // SEG_TT
    nchunks = nrows // SEG_CH
    n_items = ntiles + nchunks
    b = jnp.concatenate([base_flat[::SEG_TT].astype(I32), jnp.full((1,), nrows, I32)])
    lo = jnp.minimum(b[:-1] // SEG_CH, nchunks - 1)
    hi = jnp.maximum((b[1:] - 1) // SEG_CH, lo)
    cnt = hi - lo + 1
    ends = jnp.cumsum(cnt)
    starts = ends - cnt
    item = jnp.arange(n_items, dtype=I32)
    tile = jnp.minimum(jnp.searchsorted(ends, item, side="right").astype(I32), ntiles - 1)
    chunk = jnp.minimum(lo[tile] + item - starts[tile], nchunks - 1)
    valid = (item < ends[-1]).astype(I32)
    return tile, chunk, valid


def _moe(x1, h, aff_t, w_gate, w_up, w_down, layer):
    t = x1.shape[0]
    cap = EC_CAPACITY_FACTOR * t // N_EXPERTS
    idx, gate, pos, base = _route(aff_t, cap)
    idx_flat = idx.reshape(-1)
    xe = _gather_rows(h, idx_flat)
    ye = _expert_ffn(xe, w_gate, w_up, w_down, layer, gate.reshape(-1, 1), idx_flat.reshape(-1, 1), cap)
    ys = _scatter_rows(ye, pos.reshape(-1))
    tile, chunk, valid = _combine_plan(base.reshape(-1), ye.shape[0])
    return _combine(ys, x1, tile, chunk, valid)


def _block_diag_ones():
    i = np.arange(CHUNK) // HEAD_DIM
    return jnp.asarray((i[:, None] == i[None, :]).astype(np.float32), dtype=BF16)


def _router_split(w_router):
    w = jnp.pad(w_router.astype(F32), ((0, 0), (0, LANES - N_EXPERTS)))
    hi = w.astype(BF16)
    lo = (w - hi.astype(F32)).astype(BF16)
    return hi, lo


def _layer_a(x, rel_bias, norm_mix, w_in, q_gain, k_gain, w_out, norm_ffn, router):
    bsz, seq, d = x.shape
    qk_gain = jnp.concatenate([jnp.tile(q_gain.astype(F32), N_HEADS) * (HEAD_DIM ** -0.5),
                               jnp.tile(k_gain.astype(F32), N_HEADS),
                               jnp.ones((MIX_WIDTH,), F32)])[None]
    nq = MIX_WIDTH // CHUNK
    groups = tuple((dil, (g, nq + g, 2 * nq + g)) for g, (_, dil) in enumerate(DILATED_GROUPS))
    qkv = _project(x, norm_mix[None].astype(F32), w_in.astype(BF16), _block_diag_ones(), qk_gain,
                   n_norm_chunks=2 * nq, groups=groups)
    o_list, lse_list = [], []
    for g, (window, dil) in enumerate(DILATED_GROUPS):
        half = window // (2 * dil)
        heads = list(range(g * HEADS_PER_CHUNK, (g + 1) * HEADS_PER_CHUNK))
        bias = _bias_table(rel_bias, heads, half, min(3 * half, seq // dil), dil)
        o, lse = _banded_attention(qkv[g], qkv[g], bias, None, qb=half, gq=1, q_col=0, k_col=1, v_col=2,
                                   out_dtype=F32, want_lse=True, unroll=ATTN_UNROLL_A)
        o_list.append(o)
        lse_list.append(lse)
    r_hi, r_lo = _router_split(router)
    return _out_project(o_list, lse_list, x, w_out.astype(BF16), norm_ffn[None].astype(F32), r_hi, r_lo)


def _gqa_perm():
    g, kv, dd = np.meshgrid(np.arange(Q_PER_KV), np.arange(N_KV_HEADS), np.arange(HEAD_DIM), indexing="ij")
    return (kv * Q_PER_KV * HEAD_DIM + g * HEAD_DIM + dd).reshape(-1)


def _layer_b(x, rel_bias, norm_mix, w_in, q_gain, k_gain, sink, w_out, norm_ffn, router):
    bsz, seq, d = x.shape
    perm = _gqa_perm()
    w_in_p = jnp.concatenate([w_in[:, :MIX_WIDTH][:, perm], w_in[:, MIX_WIDTH:]], axis=1).astype(BF16)
    w_out_p = w_out[perm, :].astype(BF16)
    qk_gain = jnp.concatenate([jnp.tile(q_gain.astype(F32), N_HEADS) * (HEAD_DIM ** -0.5),
                               jnp.tile(k_gain.astype(F32), N_KV_HEADS),
                               jnp.ones((KV_WIDTH,), F32)])[None]
    nchunks = (MIX_WIDTH + 2 * KV_WIDTH) // CHUNK
    (qkv,) = _project(x, norm_mix[None].astype(F32), w_in_p, _block_diag_ones(), qk_gain,
                      n_norm_chunks=(MIX_WIDTH + KV_WIDTH) // CHUNK, groups=((1, tuple(range(nchunks))),))
    half = LOCAL_HALF_WINDOW
    heads = [kv * Q_PER_KV + g for g in range(Q_PER_KV) for kv in range(N_KV_HEADS)]
    bias = _bias_table(rel_bias, heads, half, min(3 * half, seq), 1)
    sink_tab = jnp.repeat(sink.astype(F32)[np.asarray(heads)].reshape(Q_PER_KV, N_KV_HEADS), half, axis=1)[..., None]
    (o,) = _banded_attention(qkv, qkv, bias, sink_tab, qb=half, gq=Q_PER_KV, q_col=0, k_col=MIX_WIDTH // CHUNK,
                             v_col=MIX_WIDTH // CHUNK + 1, out_dtype=BF16, want_lse=False, unroll=ATTN_UNROLL_B)
    r_hi, r_lo = _router_split(router)
    return _out_project([o], [], x, w_out_p, norm_ffn[None].astype(F32), r_hi, r_lo)


def _trunk(x, rel_bias, norm_mix, norm_ffn, a_w_in, a_q_gain, a_k_gain, a_w_out, b_w_in, b_q_gain, b_k_gain,
           b_sink, b_w_out, moe_router, moe_w_gate, moe_w_up, moe_w_down):
    shape = x.shape
    t = shape[0] * shape[1]
    for i in range(DEPTH):
        j = i // 2
        if i % 2 == 0:
            x1, h, aff_t = _layer_a(x, rel_bias, norm_mix[i], a_w_in[j], a_q_gain[j], a_k_gain[j], a_w_out[j],
                                    norm_ffn[i], moe_router[i])
        else:
            x1, h, aff_t = _layer_b(x, rel_bias, norm_mix[i], b_w_in[j], b_q_gain[j], b_k_gain[j], b_sink[j],
                                    b_w_out[j], norm_ffn[i], moe_router[i])
        x = _moe(x1.reshape(t, -1), h.reshape(t, -1), aff_t, moe_w_gate, moe_w_up, moe_w_down, i).reshape(shape)
    return x


def kernel(x_prompt, x_sample, rel_bias, norm_mix, norm_ffn, a_w_in, a_q_gain, a_k_gain, a_w_out, b_w_in, b_q_gain, b_k_gain, b_sink, b_w_out, moe_router, moe_w_gate, moe_w_up, moe_w_down):
    weights = (rel_bias, norm_mix, norm_ffn, a_w_in, a_q_gain, a_k_gain, a_w_out, b_w_in, b_q_gain, b_k_gain,
               b_sink, b_w_out, moe_router, moe_w_gate, moe_w_up, moe_w_down)
    return _trunk(x_prompt, *weights), _trunk(x_sample, *weights)
```

```python
import functools
import math

import numpy as np
import jax
import jax.numpy as jnp
from jax import lax
from jax.experimental import pallas as pl
from jax.experimental.pallas import tpu as pltpu

F32 = jnp.float32
BF16 = jnp.bfloat16
I32 = jnp.int32

D_MODEL = 1024
N_HEADS = 12
HEAD_DIM = 64
MIX_WIDTH = N_HEADS * HEAD_DIM
DILATED_GROUPS = ((128, 1), (512, 4), (2048, 16))
N_KV_HEADS = 4
Q_PER_KV = N_HEADS // N_KV_HEADS
KV_WIDTH = N_KV_HEADS * HEAD_DIM
LOCAL_HALF_WINDOW = 128
N_BUCKETS = 32
REL_MAX_DISTANCE = 1024
N_EXPERTS = 16
EC_CAPACITY_FACTOR = 2
D_EXPERT = 1024
DEPTH = 2
RMS_EPS = 1e-6
NEG_INF = -1e30

LANES = 128
HEADS_PER_CHUNK = 4
CHUNK = HEADS_PER_CHUNK * HEAD_DIM
VMEM_LIMIT = 56 * 1024 * 1024

PROJ_TM = 512
ATTN_TQ = 512
ATTN_UNROLL_A = 8
ATTN_UNROLL_B = 4
ATTN_QSUB_B = 64
FFN_TC = 512
SEG_TT = 256
SEG_CH = 256
ROW_TILE = D_MODEL // LANES
assert ROW_TILE == 8


def _cparams(sem):
    return pltpu.CompilerParams(dimension_semantics=sem, vmem_limit_bytes=VMEM_LIMIT)


def _proj_body(x_ref, g_ref, w_ref, bd_ref, qk_ref, *rest, n_norm_chunks, groups):
    o_refs = rest[:len(groups)]
    fold_scr = rest[len(groups)] if len(rest) > len(groups) else None
    x = x_ref[0]
    tm = x.shape[0]
    ms = jnp.mean(x * x, axis=-1, keepdims=True)
    h = (x * lax.rsqrt(ms + RMS_EPS) * g_ref[...]).astype(BF16)
    acc = jnp.dot(h, w_ref[...], preferred_element_type=F32)
    bd = bd_ref[...]
    for og, (r, chunks) in enumerate(groups):
        for k, c in enumerate(chunks):
            sl = slice(c * CHUNK, (c + 1) * CHUNK)
            a = acc[:, sl]
            if c < n_norm_chunks:
                a2 = a * a
                hi = a2.astype(BF16)
                lo = (a2 - hi.astype(F32)).astype(BF16)
                ss = jnp.dot(hi, bd, preferred_element_type=F32) + jnp.dot(lo, bd, preferred_element_type=F32)
                a = a * lax.rsqrt(ss * (1.0 / HEAD_DIM) + RMS_EPS) * qk_ref[:, sl]
            if r == 1:
                o_refs[og][0, 0, :, k * CHUNK:(k + 1) * CHUNK] = a.astype(BF16)
            else:
                for s in range(CHUNK // LANES):
                    fold_scr[s] = a[:, s * LANES:(s + 1) * LANES]
                for p in range(r):
                    for s in range(CHUNK // LANES):
                        lo_lane = k * CHUNK + s * LANES
                        o_refs[og][0, p, :, lo_lane:lo_lane + LANES] = (
                            fold_scr[s, pl.ds(p, tm // r, stride=r), :].astype(BF16))


def _project(x3d, gain, w_bf, bd, qk_gain, n_norm_chunks, groups):
    bsz, seq, d = x3d.shape
    n = w_bf.shape[1]
    tm = PROJ_TM
    scratch = [pltpu.VMEM((CHUNK // LANES, tm, LANES), F32)] if any(r > 1 for r, _ in groups) else []
    return pl.pallas_call(
        functools.partial(_proj_body, n_norm_chunks=n_norm_chunks, groups=groups),
        grid=(bsz, seq // tm),
        in_specs=[
            pl.BlockSpec((1, tm, d), lambda b, i: (b, i, 0)),
            pl.BlockSpec((1, d), lambda b, i: (0, 0)),
            pl.BlockSpec((d, n), lambda b, i: (0, 0)),
            pl.BlockSpec((CHUNK, CHUNK), lambda b, i: (0, 0)),
            pl.BlockSpec((1, n), lambda b, i: (0, 0)),
        ],
        out_specs=[pl.BlockSpec((1, r, tm // r, CHUNK * len(ch)), lambda b, i: (b, 0, i, 0)) for r, ch in groups],
        out_shape=[jax.ShapeDtypeStruct((bsz, r, seq // r, CHUNK * len(ch)), BF16) for r, ch in groups],
        scratch_shapes=scratch,
        compiler_params=_cparams(("parallel", "parallel")),
        name="proj",
    )(x3d, gain, w_bf, bd, qk_gain)


def _attn_body(*refs, qb, qsub, wk, n, gq, nblk, has_sink, want_lse, unroll):
    q_ref, k_ref, v_ref, bias_ref = refs[:4]
    pos = 4
    sink_ref = None
    if has_sink:
        sink_ref = refs[pos]
        pos += 1
    o_ref = refs[pos]
    lse_ref = refs[pos + 1] if want_lse else None

    jt = pl.program_id(2)
    lane_head = lax.broadcasted_iota(I32, (1, CHUNK), 1) // HEAD_DIM
    masks = [lane_head == h for h in range(HEADS_PER_CHUNK)]

    def block(blk, carry):
        i = jt * nblk + blk
        ws = jnp.clip((i - 1) * qb, 0, n - wk)
        var = i - ws // qb
        ws = pl.multiple_of(ws, qb)
        r0 = pl.multiple_of(blk * qb, qb)
        kb = k_ref[0, 0, pl.ds(ws, wk), :]
        vb = v_ref[0, 0, pl.ds(ws, wk), :]
        for g in range(gq):
            cs = slice(g * CHUNK, (g + 1) * CHUNK)
            for u in range(qb // qsub):
                rows = pl.ds(pl.multiple_of(r0 + u * qsub, qsub), qsub)
                qg = q_ref[0, 0, rows, cs]
                zero = jnp.zeros_like(qg)
                q4 = jnp.concatenate([jnp.where(m, qg, zero) for m in masks], axis=0)
                s = lax.dot_general(q4, kb, (((1,), (1,)), ((), ())), preferred_element_type=F32)
                s = s + bias_ref[var, g, :, u * qsub:(u + 1) * qsub, :].reshape(HEADS_PER_CHUNK * qsub, wk)
                m = jnp.max(s, axis=-1, keepdims=True)
                if has_sink:
                    snk = sink_ref[g]
                    m = jnp.maximum(m, snk)
                e = jnp.exp(s - m)
                l = jnp.sum(e, axis=-1, keepdims=True)
                if has_sink:
                    l = l + jnp.exp(snk - m)
                p = (e * (1.0 / l)).astype(BF16)
                pv = jnp.dot(p, vb, preferred_element_type=F32)
                o = jnp.zeros((qsub, CHUNK), F32)
                for h in range(HEADS_PER_CHUNK):
                    o = jnp.where(masks[h], pv[h * qsub:(h + 1) * qsub], o)
                o_ref[0, 0, rows, cs] = o.astype(o_ref.dtype)
                if want_lse:
                    lse = m + jnp.log(l)
                    lo = jnp.zeros((qsub, CHUNK), F32)
                    for h in range(HEADS_PER_CHUNK):
                        lo = jnp.where(masks[h], lse[h * qsub:(h + 1) * qsub], lo)
                    lse_ref[0, 0, rows, cs] = lo
        return carry

    lax.fori_loop(0, nblk, block, 0, unroll=unroll)


def _t5_bucket(rel):
    half = N_BUCKETS // 2
    max_exact = half // 2
    nn = -rel
    base = np.where(nn < 0, half, 0)
    nn = np.abs(nn)
    large = max_exact + (np.log(np.maximum(nn, 1) / max_exact)
                         / math.log(REL_MAX_DISTANCE / max_exact) * (half - max_exact)).astype(np.int32)
    large = np.minimum(large, half - 1)
    return (base + np.where(nn < max_exact, nn, large)).astype(np.int32)


def _bias_body(heads_ref, bucket_ref, rb_ref, o_ref):
    b = bucket_ref[0]
    head = heads_ref[pl.program_id(1)]

    def step(k, acc):
        return jnp.where(b == k, rb_ref[k, head], acc)

    o_ref[0, 0] = lax.fori_loop(0, N_BUCKETS, step, jnp.full(b.shape, NEG_INF, F32))


def _bias_table(rel_bias, heads, qb, wk, dil):
    var = np.arange(3)[:, None, None]
    rel = np.arange(wk)[None, None, :] - np.arange(qb)[None, :, None] - var * qb
    bucket = np.where(np.abs(rel) <= qb, _t5_bucket(rel * dil), -1).astype(np.int32)
    nh = len(heads)
    grid_spec = pltpu.PrefetchScalarGridSpec(
        num_scalar_prefetch=1,
        grid=(3, nh),
        in_specs=[
            pl.BlockSpec((1, qb, wk), lambda v, hh, hd: (v, 0, 0)),
            pl.BlockSpec(memory_space=pltpu.SMEM),
        ],
        out_specs=pl.BlockSpec((1, 1, qb, wk), lambda v, hh, hd: (v, hh, 0, 0)),
    )
    tab = pl.pallas_call(
        _bias_body,
        grid_spec=grid_spec,
        out_shape=jax.ShapeDtypeStruct((3, nh, qb, wk), F32),
        compiler_params=_cparams(("arbitrary", "arbitrary")),
        name="bias_table",
    )(jnp.asarray(np.asarray(heads, np.int32)), jnp.asarray(bucket), rel_bias.astype(F32))
    return tab.reshape(3, nh // HEADS_PER_CHUNK, HEADS_PER_CHUNK, qb, wk)


def _banded_attention(q_arr, kv_arr, bias, sink, *, qb, qsub, gq, q_col, k_col, v_col, out_dtype, want_lse,
                      unroll):
    batch, r, n, _ = q_arr.shape
    wk = min(3 * qb, n)
    tq = min(ATTN_TQ, n)
    nblk = tq // qb
    qw = gq * CHUNK
    in_specs = [
        pl.BlockSpec((1, 1, tq, qw), lambda b, p, j: (b, p, j, q_col)),
        pl.BlockSpec((1, 1, n, CHUNK), lambda b, p, j: (b, p, 0, k_col)),
        pl.BlockSpec((1, 1, n, CHUNK), lambda b, p, j: (b, p, 0, v_col)),
        pl.BlockSpec(bias.shape, lambda b, p, j: (0, 0, 0, 0, 0)),
    ]
    args = [q_arr, kv_arr, kv_arr, bias]
    if sink is not None:
        in_specs.append(pl.BlockSpec(sink.shape, lambda b, p, j: (0, 0, 0)))
        args.append(sink)
    o_spec = pl.BlockSpec((1, 1, tq, qw), lambda b, p, j: (b, p, j, 0))
    out_specs, out_shape = [o_spec], [jax.ShapeDtypeStruct((batch, r, n, qw), out_dtype)]
    if want_lse:
        out_specs.append(o_spec)
        out_shape.append(jax.ShapeDtypeStruct((batch, r, n, qw), F32))
    return pl.pallas_call(
        functools.partial(_attn_body, qb=qb, qsub=qsub, wk=wk, n=n, gq=gq, nblk=nblk, has_sink=sink is not None,
                          want_lse=want_lse, unroll=unroll),
        grid=(batch, r, n // tq),
        in_specs=in_specs,
        out_specs=out_specs,
        out_shape=out_shape,
        compiler_params=_cparams(("parallel", "parallel", "arbitrary")),
        name="attn",
    )(*args)


def _store_row_tiles(ref, val):
    rows = val.shape[0]
    for k in range(ROW_TILE):
        ref[pl.ds(k, rows, stride=ROW_TILE), :] = val[:, k * LANES:(k + 1) * LANES]


def _load_row_tiles(ref, rows, lead=()):
    return jnp.concatenate([ref[lead + (pl.ds(k, rows, stride=ROW_TILE), slice(None))] for k in range(ROW_TILE)],
                           axis=1)


def _unfold(ref, scr, r):
    if r == 1:
        return ref[0, 0]
    rows = ref.shape[2]
    for p in range(r):
        for s in range(CHUNK // LANES):
            scr[s, pl.ds(p, rows, stride=r), :] = ref[0, p, :, s * LANES:(s + 1) * LANES]
    return jnp.concatenate([scr[s] for s in range(CHUNK // LANES)], axis=1)


def _outproj_body(*refs, dils):
    n_mix = len(dils)
    if n_mix:
        o_refs, lse_refs = refs[:n_mix], refs[n_mix:2 * n_mix]
        x_ref, w_ref, g_ref, rh_ref, rl_ref, x1_ref, h_ref, aff_ref = refs[2 * n_mix:2 * n_mix + 8]
        scrs = list(refs[2 * n_mix + 8:])
        os_, ls = [], []
        for g, r in enumerate(dils):
            os_.append(_unfold(o_refs[g], scrs.pop(0) if r > 1 else None, r))
            ls.append(_unfold(lse_refs[g], scrs.pop(0) if r > 1 else None, r))
        mx = functools.reduce(jnp.maximum, ls)
        es = [jnp.exp(l - mx) for l in ls]
        inv = 1.0 / functools.reduce(lambda a, b: a + b, es)
        mixed = jnp.concatenate([(es[g] * inv) * os_[g] for g in range(n_mix)], axis=1).astype(BF16)
    else:
        o_ref, x_ref, w_ref, g_ref, rh_ref, rl_ref, x1_ref, h_ref, aff_ref = refs
        mixed = o_ref[0, 0]
    x1 = x_ref[0] + jnp.dot(mixed, w_ref[...], preferred_element_type=F32)
    x1_ref[0] = x1
    ms = jnp.mean(x1 * x1, axis=-1, keepdims=True)
    h = x1 * lax.rsqrt(ms + RMS_EPS) * g_ref[...]
    _store_row_tiles(h_ref, h)
    h_hi = h.astype(BF16)
    h_lo = (h - h_hi.astype(F32)).astype(BF16)
    rh = rh_ref[...]
    logits = (jnp.dot(h_hi, rh, preferred_element_type=F32) + jnp.dot(h_lo, rh, preferred_element_type=F32)
              + jnp.dot(h_hi, rl_ref[...], preferred_element_type=F32))
    lt = logits.T[:N_EXPERTS]
    mx = jnp.max(lt, axis=0, keepdims=True)
    e = jnp.exp(lt - mx)
    aff_ref[...] = e / jnp.sum(e, axis=0, keepdims=True)


def _out_project(o_list, lse_list, x3d, w_bf, gain, r_hi, r_lo):
    bsz, seq, d = x3d.shape
    tm = PROJ_TM
    nt = seq // tm
    dils = tuple(a.shape[1] for a in lse_list)
    args = list(o_list) + list(lse_list)
    in_specs = [pl.BlockSpec((1, a.shape[1], tm // a.shape[1], a.shape[3]), lambda b, i: (b, 0, i, 0)) for a in args]
    in_specs += [
        pl.BlockSpec((1, tm, d), lambda b, i: (b, i, 0)),
        pl.BlockSpec((w_bf.shape[0], d), lambda b, i: (0, 0)),
        pl.BlockSpec((1, d), lambda b, i: (0, 0)),
        pl.BlockSpec((d, LANES), lambda b, i: (0, 0)),
        pl.BlockSpec((d, LANES), lambda b, i: (0, 0)),
    ]
    args += [x3d, w_bf, gain, r_hi, r_lo]
    n_scr = 2 * sum(1 for r in dils if r > 1)
    return pl.pallas_call(
        functools.partial(_outproj_body, dils=dils),
        grid=(bsz, nt),
        in_specs=in_specs,
        out_specs=[
            pl.BlockSpec((1, tm, d), lambda b, i: (b, i, 0)),
            pl.BlockSpec((tm * ROW_TILE, LANES), lambda b, i: (b * nt + i, 0)),
            pl.BlockSpec((N_EXPERTS, tm), lambda b, i: (0, b * nt + i)),
        ],
        out_shape=[
            jax.ShapeDtypeStruct((bsz, seq, d), F32),
            jax.ShapeDtypeStruct((bsz * seq * ROW_TILE, LANES), F32),
            jax.ShapeDtypeStruct((N_EXPERTS, bsz * seq), F32),
        ],
        scratch_shapes=[pltpu.VMEM((CHUNK // LANES, tm, LANES), F32)] * n_scr,
        compiler_params=_cparams(("parallel", "parallel")),
        name="outproj",
    )(*args)


def _count(m):
    return jnp.sum(jnp.sum(m.astype(F32), axis=0, keepdims=True), axis=1, keepdims=True)


def _split3(x):
    hi = x.astype(BF16)
    r1 = x - hi.astype(F32)
    mid = r1.astype(BF16)
    lo = (r1 - mid.astype(F32)).astype(BF16)
    return hi, mid, lo


def _route_body(aff_ref, idx_ref, gate_ref, pos_ref, base_ref, bend_ref, mask_scr, pos_scr, *, cap, nchunk):
    ne = N_EXPERTS
    cpos = LANES
    sub_i = lax.broadcasted_iota(I32, (nchunk, cpos), 0)
    lane_i = lax.broadcasted_iota(I32, (nchunk, cpos), 1)
    tok = sub_i * cpos + lane_i
    capf = jnp.float32(cap)

    def bits(e):
        return pltpu.bitcast(aff_ref[e], I32)

    def val_step(it, taus):
        bit = jnp.left_shift(jnp.int32(1), 30 - it)
        out = []
        for e in range(ne):
            cand = taus[e] | bit
            cnt = _count(bits(e) >= cand)
            out.append(jnp.where(cnt >= capf, cand, taus[e]))
        return tuple(out)

    taus = lax.fori_loop(0, 31, val_step, tuple(jnp.zeros((1, 1), I32) for _ in range(ne)))
    need = [capf - _count(bits(e) > taus[e]) for e in range(ne)]

    tbits = int(math.log2(nchunk * cpos))

    def tie_step(it, vs):
        bit = jnp.left_shift(jnp.int32(1), tbits - 1 - it)
        out = []
        for e in range(ne):
            cand = vs[e] | bit
            cnt = _count((bits(e) == taus[e]) & (tok < cand))
            out.append(jnp.where(cnt < need[e], cand, vs[e]))
        return tuple(out)

    vs = lax.fori_loop(0, tbits, tie_step, tuple(jnp.zeros((1, 1), I32) for _ in range(ne)))

    ksum = jnp.zeros((nchunk, cpos), F32)
    for e in range(ne):
        b = bits(e)
        m = ((b > taus[e]) | ((b == taus[e]) & (tok <= vs[e]))).astype(F32)
        mask_scr[e] = m
        pos_scr[e] = ksum
        ksum = ksum + m
    tri_u = (lax.broadcasted_iota(I32, (cpos, cpos), 0)
             <= lax.broadcasted_iota(I32, (cpos, cpos), 1)).astype(BF16)
    cs_i = lax.broadcasted_iota(I32, (nchunk, nchunk), 0)
    cl_i = lax.broadcasted_iota(I32, (nchunk, nchunk), 1)
    tri_ls = (cl_i < cs_i).astype(BF16)
    tri_li = (cl_i <= cs_i).astype(BF16)
    kin = jnp.dot(ksum.astype(BF16), tri_u, preferred_element_type=F32)
    ktot = jnp.broadcast_to(kin[:, cpos - 1:cpos], (nchunk, cpos))
    khi = jnp.floor(ktot * (1.0 / 256.0))
    klo = ktot - 256.0 * khi
    kstart = (256.0 * jnp.dot(tri_ls, khi.astype(BF16), preferred_element_type=F32)
              + jnp.dot(tri_ls, klo.astype(BF16), preferred_element_type=F32))
    base = kstart + kin - ksum
    base_ref[...] = base
    bend_ref[...] = base + ksum

    s_row = lax.broadcasted_iota(I32, (1, cap), 1).astype(F32)
    c_col = lax.broadcasted_iota(I32, (nchunk, cap), 0).astype(F32)
    j_col = lax.broadcasted_iota(I32, (cpos, cap), 0).astype(F32)
    reps = cap // cpos
    for e in range(ne):
        m = mask_scr[e]
        posv = pos_scr[e] + base
        lin = jnp.dot(m.astype(BF16), tri_u, preferred_element_type=F32)
        tot = jnp.broadcast_to(lin[:, cpos - 1:cpos], (nchunk, cpos))
        cend = jnp.dot(tri_li, tot.astype(BF16), preferred_element_type=F32)
        cstart = cend - tot
        cend_w = jnp.concatenate([cend] * reps, axis=1)
        cstart_w = jnp.concatenate([cstart] * reps, axis=1)
        cidx = jnp.sum((cend_w <= s_row).astype(F32), axis=0, keepdims=True)
        oh = c_col == cidx
        ohb = oh.astype(BF16)
        cst = jnp.sum(jnp.where(oh, cstart_w, 0.0), axis=0, keepdims=True)
        target = s_row - cst
        g = jnp.dot(lin.T.astype(BF16), ohb, preferred_element_type=F32)
        jidx = jnp.sum((g <= target).astype(F32), axis=0, keepdims=True)
        ohj = j_col == jidx
        idx_ref[e:e + 1, :] = (cidx * float(cpos) + jidx).astype(I32)
        a_hi, a_mid, a_lo = _split3(aff_ref[e].T)
        ga = (jnp.dot(a_hi, ohb, preferred_element_type=F32) + jnp.dot(a_mid, ohb, preferred_element_type=F32)
              + jnp.dot(a_lo, ohb, preferred_element_type=F32))
        gate_ref[e:e + 1, :] = jnp.sum(jnp.where(ohj, ga, 0.0), axis=0, keepdims=True)
        pt = posv.T
        p_hi = jnp.floor(pt * (1.0 / 256.0))
        p_lo = pt - 256.0 * p_hi
        gp = (256.0 * jnp.dot(p_hi.astype(BF16), ohb, preferred_element_type=F32)
              + jnp.dot(p_lo.astype(BF16), ohb, preferred_element_type=F32))
        pos_ref[e:e + 1, :] = jnp.sum(jnp.where(ohj, gp, 0.0), axis=0, keepdims=True).astype(I32)


def _route(aff_t, cap):
    ne, t = aff_t.shape
    nchunk = t // LANES
    aff3 = aff_t.reshape(ne, nchunk, LANES)
    return pl.pallas_call(
        functools.partial(_route_body, cap=cap, nchunk=nchunk),
        in_specs=[pl.BlockSpec(memory_space=pltpu.VMEM)],
        out_specs=[pl.BlockSpec(memory_space=pltpu.VMEM)] * 5,
        out_shape=[
            jax.ShapeDtypeStruct((ne, cap), I32),
            jax.ShapeDtypeStruct((ne, cap), F32),
            jax.ShapeDtypeStruct((ne, cap), I32),
            jax.ShapeDtypeStruct((nchunk, LANES), F32),
            jax.ShapeDtypeStruct((nchunk, LANES), F32),
        ],
        scratch_shapes=[pltpu.VMEM((ne, nchunk, LANES), F32), pltpu.VMEM((ne, nchunk, LANES), F32)],
        compiler_params=pltpu.CompilerParams(vmem_limit_bytes=VMEM_LIMIT),
        name="route",
    )(aff3)


def _ffn_body(idx_ref, idxn_ref, pos_ref, gate_ref, wg_ref, wu_ref, wd_ref, h_hbm, y_hbm,
              xbuf, ybuf, gsem, ssem, wg_s, wu_s, wd_s, *, tc, nsteps):
    e = pl.program_id(0)
    j = pl.program_id(1)
    s = e * pl.num_programs(1) + j
    slot = lax.rem(s, 2)

    def tile(ref, row):
        return ref.at[pl.ds(pl.multiple_of(row * ROW_TILE, ROW_TILE), ROW_TILE), :]

    def gather_copy(tok, i, sl):
        return pltpu.make_async_copy(tile(h_hbm, tok), tile(xbuf.at[sl], i), gsem.at[sl])

    def scatter_copy(row, i, sl):
        return pltpu.make_async_copy(tile(ybuf.at[sl], i), tile(y_hbm, row), ssem.at[sl])

    def start_gather(ref, sl):
        def body(i, c):
            gather_copy(ref[0, 0, i], i, sl).start()
            return c
        lax.fori_loop(0, tc, body, 0, unroll=8)

    def wait_scatter(sl):
        def body(i, c):
            scatter_copy(0, i, sl).wait()
            return c
        lax.fori_loop(0, tc, body, 0, unroll=8)

    @pl.when(s == 0)
    def _():
        start_gather(idx_ref, 0)

    @pl.when(s + 1 < nsteps)
    def _():
        start_gather(idxn_ref, 1 - slot)

    @pl.when(j == 0)
    def _():
        wg_s[...] = wg_ref[0, 0].astype(BF16)
        wu_s[...] = wu_ref[0, 0].astype(BF16)
        wd_s[...] = wd_ref[0, 0].astype(BF16)

    def wait_g(i, c):
        gather_copy(0, i, slot).wait()
        return c
    lax.fori_loop(0, tc, wait_g, 0, unroll=8)

    x = _load_row_tiles(xbuf, tc, lead=(slot,)).astype(BF16)
    g = jnp.dot(x, wg_s[...], preferred_element_type=F32)
    u = jnp.dot(x, wu_s[...], preferred_element_type=F32)
    hid = (g * (1.0 / (1.0 + jnp.exp(-g))) * u).astype(BF16)
    y = jnp.dot(hid, wd_s[...], preferred_element_type=F32) * gate_ref[...]

    @pl.when(s >= 2)
    def _():
        wait_scatter(slot)

    _store_row_tiles(ybuf.at[slot], y)

    def start_s(i, c):
        scatter_copy(pos_ref[0, 0, i], i, slot).start()
        return c
    lax.fori_loop(0, tc, start_s, 0, unroll=8)

    @pl.when(s == nsteps - 1)
    def _():
        wait_scatter(1 - slot)
        wait_scatter(slot)


def _expert_ffn(h_tiles, idx, pos, gate_col, w_gate, w_up, w_down, layer):
    _, ne, d, f = w_gate.shape
    cap = idx.shape[1]
    tc = FFN_TC
    nc = cap // tc
    nsteps = ne * nc
    assert nsteps >= 2
    idx3 = idx.reshape(nsteps, 1, tc)
    pos3 = pos.reshape(nsteps, 1, tc)
    smem = functools.partial(pl.BlockSpec, (1, 1, tc), memory_space=pltpu.SMEM)
    return pl.pallas_call(
        functools.partial(_ffn_body, tc=tc, nsteps=nsteps),
        grid=(ne, nc),
        in_specs=[
            smem(lambda e, j: (e * nc + j, 0, 0)),
            smem(lambda e, j: (jnp.minimum(e * nc + j + 1, nsteps - 1), 0, 0)),
            smem(lambda e, j: (e * nc + j, 0, 0)),
            pl.BlockSpec((tc, 1), lambda e, j: (e * nc + j, 0)),
            pl.BlockSpec((1, 1, d, f), lambda e, j: (layer, e, 0, 0)),
            pl.BlockSpec((1, 1, d, f), lambda e, j: (layer, e, 0, 0)),
            pl.BlockSpec((1, 1, f, d), lambda e, j: (layer, e, 0, 0)),
            pl.BlockSpec(memory_space=pl.ANY),
        ],
        out_specs=pl.BlockSpec(memory_space=pl.ANY),
        out_shape=jax.ShapeDtypeStruct((ne * cap * ROW_TILE, LANES), F32),
        scratch_shapes=[
            pltpu.VMEM((2, tc * ROW_TILE, LANES), F32),
            pltpu.VMEM((2, tc * ROW_TILE, LANES), F32),
            pltpu.SemaphoreType.DMA((2,)),
            pltpu.SemaphoreType.DMA((2,)),
            pltpu.VMEM((d, f), BF16), pltpu.VMEM((d, f), BF16), pltpu.VMEM((f, d), BF16),
        ],
        compiler_params=_cparams(("arbitrary", "arbitrary")),
        name="expert_ffn",
    )(idx3, idx3, pos3, gate_col, w_gate, w_up, w_down, h_tiles)


def _combine_body(tile_ref, chunk_ref, valid_ref, y_ref, base_ref, bend_ref, x_ref, o_ref):
    i = pl.program_id(0)
    t = tile_ref[i]
    prev = tile_ref[jnp.maximum(i - 1, 0)]

    @pl.when((i == 0) | (prev != t))
    def _():
        o_ref[...] = x_ref[...]

    @pl.when(valid_ref[i] != 0)
    def _():
        tt = o_ref.shape[0]
        rows = y_ref.shape[0] // ROW_TILE
        y = _load_row_tiles(y_ref, rows)
        a = (lax.broadcasted_iota(I32, (rows, tt), 0) + chunk_ref[i] * rows).astype(F32)
        pt = ((a >= base_ref[0]).astype(F32) - (a >= bend_ref[0]).astype(F32)).astype(BF16)
        y_hi = y.astype(BF16)
        y_lo = (y - y_hi.astype(F32)).astype(BF16)
        dn = (((0,), (0,)), ((), ()))
        o_ref[...] += (lax.dot_general(pt, y_hi, dn, preferred_element_type=F32)
                       + lax.dot_general(pt, y_lo, dn, preferred_element_type=F32))


def _combine(y_tiles, base, bend, x1, tile_id, chunk_id, valid):
    t, d = x1.shape
    n_items = tile_id.shape[0]
    ntiles = t // SEG_TT
    row_spec = pl.BlockSpec((1, 1, SEG_TT), lambda i, tl, ch, va: (tl[i], 0, 0))
    grid_spec = pltpu.PrefetchScalarGridSpec(
        num_scalar_prefetch=3,
        grid=(n_items,),
        in_specs=[
            pl.BlockSpec((SEG_CH * ROW_TILE, LANES), lambda i, tl, ch, va: (ch[i], 0)),
            row_spec,
            row_spec,
            pl.BlockSpec((SEG_TT, d), lambda i, tl, ch, va: (tl[i], 0)),
        ],
        out_specs=pl.BlockSpec((SEG_TT, d), lambda i, tl, ch, va: (tl[i], 0)),
    )
    return pl.pallas_call(
        _combine_body,
        grid_spec=grid_spec,
        out_shape=jax.ShapeDtypeStruct((t, d), F32),
        compiler_params=_cparams(("arbitrary",)),
        name="combine",
    )(tile_id, chunk_id, valid, y_tiles, base.reshape(ntiles, 1, SEG_TT), bend.reshape(ntiles, 1, SEG_TT), x1)


def _combine_plan(base_flat, nrows):
    t = base_flat.shape[0]
    ntiles = t // SEG_TT
    nchunks = nrows // SEG_CH
    n_items = ntiles + nchunks
    b = jnp.concatenate([base_flat[::SEG_TT].astype(I32), jnp.full((1,), nrows, I32)])
    lo = jnp.minimum(b[:-1] // SEG_CH, nchunks - 1)
    hi = jnp.maximum((b[1:] - 1) // SEG_CH, lo)
    cnt = hi - lo + 1
    ends = jnp.cumsum(cnt)
    starts = ends - cnt
    item = jnp.arange(n_items, dtype=I32)
    tile = jnp.minimum(jnp.searchsorted(ends, item, side="right").astype(I32), ntiles - 1)
    chunk = jnp.minimum(lo[tile] + item - starts[tile], nchunks - 1)
    valid = (item < ends[-1]).astype(I32)
    return tile, chunk, valid


def _moe(x1, h, aff_t, w_gate, w_up, w_down, layer):
    t = x1.shape[0]
    cap = EC_CAPACITY_FACTOR * t // N_EXPERTS
    idx, gate, pos, base, bend = _route(aff_t, cap)
    ys = _expert_ffn(h, idx, pos, gate.reshape(-1, 1), w_gate, w_up, w_down, layer)
    base, bend = base.reshape(-1), bend.reshape(-1)
    tile, chunk, valid = _combine_plan(base, N_EXPERTS * cap)
    return _combine(ys, base, bend, x1, tile, chunk, valid)


def _block_diag_ones():
    i = np.arange(CHUNK) // HEAD_DIM
    return jnp.asarray((i[:, None] == i[None, :]).astype(np.float32), dtype=BF16)


def _router_split(w_router):
    w = jnp.pad(w_router.astype(F32), ((0, 0), (0, LANES - N_EXPERTS)))
    hi = w.astype(BF16)
    lo = (w - hi.astype(F32)).astype(BF16)
    return hi, lo


def _layer_a(x, rel_bias, norm_mix, w_in, q_gain, k_gain, w_out, norm_ffn, router):
    bsz, seq, d = x.shape
    qk_gain = jnp.concatenate([jnp.tile(q_gain.astype(F32), N_HEADS) * (HEAD_DIM ** -0.5),
                               jnp.tile(k_gain.astype(F32), N_HEADS),
                               jnp.ones((MIX_WIDTH,), F32)])[None]
    nq = MIX_WIDTH // CHUNK
    groups = tuple((dil, (g, nq + g, 2 * nq + g)) for g, (_, dil) in enumerate(DILATED_GROUPS))
    qkv = _project(x, norm_mix[None].astype(F32), w_in.astype(BF16), _block_diag_ones(), qk_gain,
                   n_norm_chunks=2 * nq, groups=groups)
    o_list, lse_list = [], []
    for g, (window, dil) in enumerate(DILATED_GROUPS):
        half = window // (2 * dil)
        heads = list(range(g * HEADS_PER_CHUNK, (g + 1) * HEADS_PER_CHUNK))
        bias = _bias_table(rel_bias, heads, half, min(3 * half, seq // dil), dil)
        o, lse = _banded_attention(qkv[g], qkv[g], bias, None, qb=half, qsub=half, gq=1, q_col=0, k_col=1, v_col=2,
                                   out_dtype=F32, want_lse=True, unroll=ATTN_UNROLL_A)
        o_list.append(o)
        lse_list.append(lse)
    r_hi, r_lo = _router_split(router)
    return _out_project(o_list, lse_list, x, w_out.astype(BF16), norm_ffn[None].astype(F32), r_hi, r_lo)


def _gqa_perm():
    g, kv, dd = np.meshgrid(np.arange(Q_PER_KV), np.arange(N_KV_HEADS), np.arange(HEAD_DIM), indexing="ij")
    return (kv * Q_PER_KV * HEAD_DIM + g * HEAD_DIM + dd).reshape(-1)


def _layer_b(x, rel_bias, norm_mix, w_in, q_gain, k_gain, sink, w_out, norm_ffn, router):
    bsz, seq, d = x.shape
    perm = _gqa_perm()
    w_in_p = jnp.concatenate([w_in[:, :MIX_WIDTH][:, perm], w_in[:, MIX_WIDTH:]], axis=1).astype(BF16)
    w_out_p = w_out[perm, :].astype(BF16)
    qk_gain = jnp.concatenate([jnp.tile(q_gain.astype(F32), N_HEADS) * (HEAD_DIM ** -0.5),
                               jnp.tile(k_gain.astype(F32), N_KV_HEADS),
                               jnp.ones((KV_WIDTH,), F32)])[None]
    nchunks = (MIX_WIDTH + 2 * KV_WIDTH) // CHUNK
    (qkv,) = _project(x, norm_mix[None].astype(F32), w_in_p, _block_diag_ones(), qk_gain,
                      n_norm_chunks=(MIX_WIDTH + KV_WIDTH) // CHUNK, groups=((1, tuple(range(nchunks))),))
    half = LOCAL_HALF_WINDOW
    heads = [kv * Q_PER_KV + g for g in range(Q_PER_KV) for kv in range(N_KV_HEADS)]
    bias = _bias_table(rel_bias, heads, half, min(3 * half, seq), 1)
    sink_tab = jnp.repeat(sink.astype(F32)[np.asarray(heads)].reshape(Q_PER_KV, N_KV_HEADS), ATTN_QSUB_B,
                          axis=1)[..., None]
    (o,) = _banded_attention(qkv, qkv, bias, sink_tab, qb=half, qsub=ATTN_QSUB_B, gq=Q_PER_KV, q_col=0,
                             k_col=MIX_WIDTH // CHUNK,
                             v_col=MIX_WIDTH // CHUNK + 1, out_dtype=BF16, want_lse=False, unroll=ATTN_UNROLL_B)
    r_hi, r_lo = _router_split(router)
    return _out_project([o], [], x, w_out_p, norm_ffn[None].astype(F32), r_hi, r_lo)


def _trunk(x, rel_bias, norm_mix, norm_ffn, a_w_in, a_q_gain, a_k_gain, a_w_out, b_w_in, b_q_gain, b_k_gain,
           b_sink, b_w_out, moe_router, moe_w_gate, moe_w_up, moe_w_down):
    shape = x.shape
    t = shape[0] * shape[1]
    for i in range(DEPTH):
        j = i // 2
        if i % 2 == 0:
            x1, h, aff_t = _layer_a(x, rel_bias, norm_mix[i], a_w_in[j], a_q_gain[j], a_k_gain[j], a_w_out[j],
                                    norm_ffn[i], moe_router[i])
        else:
            x1, h, aff_t = _layer_b(x, rel_bias, norm_mix[i], b_w_in[j], b_q_gain[j], b_k_gain[j], b_sink[j],
                                    b_w_out[j], norm_ffn[i], moe_router[i])
        x = _moe(x1.reshape(t, -1), h, aff_t, moe_w_gate, moe_w_up, moe_w_down, i).reshape(shape)
    return x


def kernel(x_prompt, x_sample, rel_bias, norm_mix, norm_ffn, a_w_in, a_q_gain, a_k_gain, a_w_out, b_w_in, b_q_gain, b_k_gain, b_sink, b_w_out, moe_router, moe_w_gate, moe_w_up, moe_w_down):
    weights = (rel_bias, norm_mix, norm_ffn, a_w_in, a_q_gain, a_k_gain, a_w_out, b_w_in, b_q_gain, b_k_gain,
               b_sink, b_w_out, moe_router, moe_w_gate, moe_w_up, moe_w_down)
    return _trunk(x_prompt, *weights), _trunk(x_sample, *weights)
```

```python
import functools
import math

import numpy as np
import jax
import jax.numpy as jnp
from jax import lax
from jax.experimental import pallas as pl
from jax.experimental.pallas import tpu as pltpu

F32 = jnp.float32
BF16 = jnp.bfloat16
I32 = jnp.int32

D_MODEL = 1024
N_HEADS = 12
HEAD_DIM = 64
MIX_WIDTH = N_HEADS * HEAD_DIM
DILATED_GROUPS = ((128, 1), (512, 4), (2048, 16))
N_KV_HEADS = 4
Q_PER_KV = N_HEADS // N_KV_HEADS
KV_WIDTH = N_KV_HEADS * HEAD_DIM
LOCAL_HALF_WINDOW = 128
N_BUCKETS = 32
REL_MAX_DISTANCE = 1024
N_EXPERTS = 16
EC_CAPACITY_FACTOR = 2
D_EXPERT = 1024
DEPTH = 2
RMS_EPS = 1e-6
NEG_INF = -1e30

LANES = 128
HEADS_PER_CHUNK = 4
CHUNK = HEADS_PER_CHUNK * HEAD_DIM
VMEM_LIMIT = 56 * 1024 * 1024

PROJ_TM = 512
ATTN_TQ = 512
ATTN_UNROLL_A = 8
ATTN_UNROLL_B = 4
ATTN_QSUB_B = 64
FFN_TC = 512
SEG_TT = 256
SEG_CH = 256
ROW_TILE = D_MODEL // LANES
assert ROW_TILE == 8


def _cparams(sem):
    return pltpu.CompilerParams(dimension_semantics=sem, vmem_limit_bytes=VMEM_LIMIT)


def _proj_body(x_ref, g_ref, w_ref, bd_ref, qk_ref, *rest, n_norm_chunks, groups):
    o_refs = rest[:len(groups)]
    fold_scr = rest[len(groups)] if len(rest) > len(groups) else None
    x = x_ref[0]
    tm = x.shape[0]
    ms = jnp.mean(x * x, axis=-1, keepdims=True)
    h = (x * lax.rsqrt(ms + RMS_EPS) * g_ref[...]).astype(BF16)
    acc = jnp.dot(h, w_ref[...], preferred_element_type=F32)
    bd = bd_ref[...]
    for og, (r, chunks) in enumerate(groups):
        for k, c in enumerate(chunks):
            sl = slice(c * CHUNK, (c + 1) * CHUNK)
            a = acc[:, sl]
            if c < n_norm_chunks:
                a2 = a * a
                hi = a2.astype(BF16)
                lo = (a2 - hi.astype(F32)).astype(BF16)
                ss = jnp.dot(hi, bd, preferred_element_type=F32) + jnp.dot(lo, bd, preferred_element_type=F32)
                a = a * lax.rsqrt(ss * (1.0 / HEAD_DIM) + RMS_EPS) * qk_ref[:, sl]
            if r == 1:
                o_refs[og][0, 0, :, k * CHUNK:(k + 1) * CHUNK] = a.astype(BF16)
            else:
                for s in range(CHUNK // LANES):
                    fold_scr[s] = a[:, s * LANES:(s + 1) * LANES]
                for p in range(r):
                    for s in range(CHUNK // LANES):
                        lo_lane = k * CHUNK + s * LANES
                        o_refs[og][0, p, :, lo_lane:lo_lane + LANES] = (
                            fold_scr[s, pl.ds(p, tm // r, stride=r), :].astype(BF16))


def _project(x3d, gain, w_bf, bd, qk_gain, n_norm_chunks, groups):
    bsz, seq, d = x3d.shape
    n = w_bf.shape[1]
    tm = PROJ_TM
    scratch = [pltpu.VMEM((CHUNK // LANES, tm, LANES), F32)] if any(r > 1 for r, _ in groups) else []
    return pl.pallas_call(
        functools.partial(_proj_body, n_norm_chunks=n_norm_chunks, groups=groups),
        grid=(bsz, seq // tm),
        in_specs=[
            pl.BlockSpec((1, tm, d), lambda b, i: (b, i, 0)),
            pl.BlockSpec((1, d), lambda b, i: (0, 0)),
            pl.BlockSpec((d, n), lambda b, i: (0, 0)),
            pl.BlockSpec((CHUNK, CHUNK), lambda b, i: (0, 0)),
            pl.BlockSpec((1, n), lambda b, i: (0, 0)),
        ],
        out_specs=[pl.BlockSpec((1, r, tm // r, CHUNK * len(ch)), lambda b, i: (b, 0, i, 0)) for r, ch in groups],
        out_shape=[jax.ShapeDtypeStruct((bsz, r, seq // r, CHUNK * len(ch)), BF16) for r, ch in groups],
        scratch_shapes=scratch,
        compiler_params=_cparams(("parallel", "parallel")),
        name="proj",
    )(x3d, gain, w_bf, bd, qk_gain)


def _attn_body(*refs, qb, qsub, wk, n, gq, nblk, pp, has_sink, want_lse, unroll):
    q_ref, k_ref, v_ref, bias_ref = refs[:4]
    pos = 4
    sink_ref = None
    if has_sink:
        sink_ref = refs[pos]
        pos += 1
    o_ref = refs[pos]
    lse_ref = refs[pos + 1] if want_lse else None

    jt = pl.program_id(2)
    lane_head = lax.broadcasted_iota(I32, (1, CHUNK), 1) // HEAD_DIM
    masks = [lane_head == h for h in range(HEADS_PER_CHUNK)]

    def block(it, carry):
        ph = it // nblk
        blk = it % nblk
        i = jt * nblk + blk
        ws = jnp.clip((i - 1) * qb, 0, n - wk)
        var = i - ws // qb
        ws = pl.multiple_of(ws, qb)
        r0 = pl.multiple_of(blk * qb, qb)
        kb = k_ref[0, ph, pl.ds(ws, wk), :]
        vb = v_ref[0, ph, pl.ds(ws, wk), :]
        for g in range(gq):
            cs = slice(g * CHUNK, (g + 1) * CHUNK)
            for u in range(qb // qsub):
                rows = pl.ds(pl.multiple_of(r0 + u * qsub, qsub), qsub)
                qg = q_ref[0, ph, rows, cs]
                zero = jnp.zeros_like(qg)
                q4 = jnp.concatenate([jnp.where(m, qg, zero) for m in masks], axis=0)
                s = lax.dot_general(q4, kb, (((1,), (1,)), ((), ())), preferred_element_type=F32)
                s = s + bias_ref[var, g, :, u * qsub:(u + 1) * qsub, :].reshape(HEADS_PER_CHUNK * qsub, wk)
                m = jnp.max(s, axis=-1, keepdims=True)
                if has_sink:
                    snk = sink_ref[g]
                    m = jnp.maximum(m, snk)
                e = jnp.exp(s - m)
                l = jnp.sum(e, axis=-1, keepdims=True)
                if has_sink:
                    l = l + jnp.exp(snk - m)
                p = (e * (1.0 / l)).astype(BF16)
                pv = jnp.dot(p, vb, preferred_element_type=F32)
                o = jnp.zeros((qsub, CHUNK), F32)
                for h in range(HEADS_PER_CHUNK):
                    o = jnp.where(masks[h], pv[h * qsub:(h + 1) * qsub], o)
                o_ref[0, ph, rows, cs] = o.astype(o_ref.dtype)
                if want_lse:
                    lse = m + jnp.log(l)
                    lo = jnp.zeros((qsub, CHUNK), F32)
                    for h in range(HEADS_PER_CHUNK):
                        lo = jnp.where(masks[h], lse[h * qsub:(h + 1) * qsub], lo)
                    lse_ref[0, ph, rows, cs] = lo
        return carry

    lax.fori_loop(0, pp * nblk, block, 0, unroll=min(unroll, pp * nblk))


def _t5_bucket(rel):
    half = N_BUCKETS // 2
    max_exact = half // 2
    nn = -rel
    base = np.where(nn < 0, half, 0)
    nn = np.abs(nn)
    large = max_exact + (np.log(np.maximum(nn, 1) / max_exact)
                         / math.log(REL_MAX_DISTANCE / max_exact) * (half - max_exact)).astype(np.int32)
    large = np.minimum(large, half - 1)
    return (base + np.where(nn < max_exact, nn, large)).astype(np.int32)


def _bias_body(heads_ref, bucket_ref, rb_ref, o_ref):
    b = bucket_ref[0]
    head = heads_ref[pl.program_id(1)]

    def step(k, acc):
        return jnp.where(b == k, rb_ref[k, head], acc)

    o_ref[0, 0] = lax.fori_loop(0, N_BUCKETS, step, jnp.full(b.shape, NEG_INF, F32))


def _bias_table(rel_bias, heads, qb, wk, dil):
    var = np.arange(3)[:, None, None]
    rel = np.arange(wk)[None, None, :] - np.arange(qb)[None, :, None] - var * qb
    bucket = np.where(np.abs(rel) <= qb, _t5_bucket(rel * dil), -1).astype(np.int32)
    nh = len(heads)
    grid_spec = pltpu.PrefetchScalarGridSpec(
        num_scalar_prefetch=1,
        grid=(3, nh),
        in_specs=[
            pl.BlockSpec((1, qb, wk), lambda v, hh, hd: (v, 0, 0)),
            pl.BlockSpec(memory_space=pltpu.SMEM),
        ],
        out_specs=pl.BlockSpec((1, 1, qb, wk), lambda v, hh, hd: (v, hh, 0, 0)),
    )
    tab = pl.pallas_call(
        _bias_body,
        grid_spec=grid_spec,
        out_shape=jax.ShapeDtypeStruct((3, nh, qb, wk), F32),
        compiler_params=_cparams(("arbitrary", "arbitrary")),
        name="bias_table",
    )(jnp.asarray(np.asarray(heads, np.int32)), jnp.asarray(bucket), rel_bias.astype(F32))
    return tab.reshape(3, nh // HEADS_PER_CHUNK, HEADS_PER_CHUNK, qb, wk)


def _banded_attention(q_arr, kv_arr, bias, sink, *, qb, qsub, gq, q_col, k_col, v_col, out_dtype, want_lse,
                      unroll):
    batch, r, n, _ = q_arr.shape
    wk = min(3 * qb, n)
    tq = min(ATTN_TQ, n)
    nblk = tq // qb
    pp = max(1, min(r, ATTN_TQ // tq))
    qw = gq * CHUNK
    in_specs = [
        pl.BlockSpec((1, pp, tq, qw), lambda b, p, j: (b, p, j, q_col)),
        pl.BlockSpec((1, pp, n, CHUNK), lambda b, p, j: (b, p, 0, k_col)),
        pl.BlockSpec((1, pp, n, CHUNK), lambda b, p, j: (b, p, 0, v_col)),
        pl.BlockSpec(bias.shape, lambda b, p, j: (0, 0, 0, 0, 0)),
    ]
    args = [q_arr, kv_arr, kv_arr, bias]
    if sink is not None:
        in_specs.append(pl.BlockSpec(sink.shape, lambda b, p, j: (0, 0, 0)))
        args.append(sink)
    o_spec = pl.BlockSpec((1, pp, tq, qw), lambda b, p, j: (b, p, j, 0))
    out_specs, out_shape = [o_spec], [jax.ShapeDtypeStruct((batch, r, n, qw), out_dtype)]
    if want_lse:
        out_specs.append(o_spec)
        out_shape.append(jax.ShapeDtypeStruct((batch, r, n, qw), F32))
    return pl.pallas_call(
        functools.partial(_attn_body, qb=qb, qsub=qsub, wk=wk, n=n, gq=gq, nblk=nblk, pp=pp,
                          has_sink=sink is not None,
                          want_lse=want_lse, unroll=unroll),
        grid=(batch, r // pp, n // tq),
        in_specs=in_specs,
        out_specs=out_specs,
        out_shape=out_shape,
        compiler_params=_cparams(("parallel", "parallel", "arbitrary")),
        name="attn",
    )(*args)


def _store_row_tiles(ref, val):
    rows = val.shape[0]
    for k in range(ROW_TILE):
        ref[pl.ds(k, rows, stride=ROW_TILE), :] = val[:, k * LANES:(k + 1) * LANES]


def _load_row_tiles(ref, rows, lead=()):
    return jnp.concatenate([ref[lead + (pl.ds(k, rows, stride=ROW_TILE), slice(None))] for k in range(ROW_TILE)],
                           axis=1)


def _unfold(ref, scr, r):
    if r == 1:
        return ref[0, 0]
    rows = ref.shape[2]
    for p in range(r):
        for s in range(CHUNK // LANES):
            scr[s, pl.ds(p, rows, stride=r), :] = ref[0, p, :, s * LANES:(s + 1) * LANES]
    return jnp.concatenate([scr[s] for s in range(CHUNK // LANES)], axis=1)


def _outproj_body(*refs, dils):
    n_mix = len(dils)
    if n_mix:
        o_refs, lse_refs = refs[:n_mix], refs[n_mix:2 * n_mix]
        x_ref, w_ref, g_ref, rh_ref, rl_ref, x1_ref, aff_ref = refs[2 * n_mix:2 * n_mix + 7]
        scrs = list(refs[2 * n_mix + 7:])
        os_, ls = [], []
        for g, r in enumerate(dils):
            os_.append(_unfold(o_refs[g], scrs.pop(0) if r > 1 else None, r))
            ls.append(_unfold(lse_refs[g], scrs.pop(0) if r > 1 else None, r))
        mx = functools.reduce(jnp.maximum, ls)
        es = [jnp.exp(l - mx) for l in ls]
        inv = 1.0 / functools.reduce(lambda a, b: a + b, es)
        mixed = jnp.concatenate([(es[g] * inv) * os_[g] for g in range(n_mix)], axis=1).astype(BF16)
    else:
        o_ref, x_ref, w_ref, g_ref, rh_ref, rl_ref, x1_ref, aff_ref = refs
        mixed = o_ref[0, 0]
    x1 = x_ref[0] + jnp.dot(mixed, w_ref[...], preferred_element_type=F32)
    _store_row_tiles(x1_ref, x1)
    ms = jnp.mean(x1 * x1, axis=-1, keepdims=True)
    h = x1 * lax.rsqrt(ms + RMS_EPS) * g_ref[...]
    h_hi = h.astype(BF16)
    h_lo = (h - h_hi.astype(F32)).astype(BF16)
    rh = rh_ref[...]
    logits = (jnp.dot(h_hi, rh, preferred_element_type=F32) + jnp.dot(h_lo, rh, preferred_element_type=F32)
              + jnp.dot(h_hi, rl_ref[...], preferred_element_type=F32))
    lt = logits.T[:N_EXPERTS]
    mx = jnp.max(lt, axis=0, keepdims=True)
    e = jnp.exp(lt - mx)
    aff_ref[...] = e / jnp.sum(e, axis=0, keepdims=True)


def _out_project(o_list, lse_list, x3d, w_bf, gain, r_hi, r_lo):
    bsz, seq, d = x3d.shape
    tm = PROJ_TM
    nt = seq // tm
    dils = tuple(a.shape[1] for a in lse_list)
    args = list(o_list) + list(lse_list)
    in_specs = [pl.BlockSpec((1, a.shape[1], tm // a.shape[1], a.shape[3]), lambda b, i: (b, 0, i, 0)) for a in args]
    in_specs += [
        pl.BlockSpec((1, tm, d), lambda b, i: (b, i, 0)),
        pl.BlockSpec((w_bf.shape[0], d), lambda b, i: (0, 0)),
        pl.BlockSpec((1, d), lambda b, i: (0, 0)),
        pl.BlockSpec((d, LANES), lambda b, i: (0, 0)),
        pl.BlockSpec((d, LANES), lambda b, i: (0, 0)),
    ]
    args += [x3d, w_bf, gain, r_hi, r_lo]
    n_scr = 2 * sum(1 for r in dils if r > 1)
    return pl.pallas_call(
        functools.partial(_outproj_body, dils=dils),
        grid=(bsz, nt),
        in_specs=in_specs,
        out_specs=[
            pl.BlockSpec((tm * ROW_TILE, LANES), lambda b, i: (b * nt + i, 0)),
            pl.BlockSpec((N_EXPERTS, tm), lambda b, i: (0, b * nt + i)),
        ],
        out_shape=[
            jax.ShapeDtypeStruct((bsz * seq * ROW_TILE, LANES), F32),
            jax.ShapeDtypeStruct((N_EXPERTS, bsz * seq), F32),
        ],
        scratch_shapes=[pltpu.VMEM((CHUNK // LANES, tm, LANES), F32)] * n_scr,
        compiler_params=_cparams(("parallel", "parallel")),
        name="outproj",
    )(*args)


def _count(m):
    return jnp.sum(jnp.sum(m.astype(F32), axis=0, keepdims=True), axis=1, keepdims=True)


def _split3(x):
    hi = x.astype(BF16)
    r1 = x - hi.astype(F32)
    mid = r1.astype(BF16)
    lo = (r1 - mid.astype(F32)).astype(BF16)
    return hi, mid, lo


def _route_body(aff_ref, idx_ref, gate_ref, pos_ref, base_ref, bend_ref, mask_scr, pos_scr, *, cap, nchunk):
    ne = N_EXPERTS
    cpos = LANES
    sub_i = lax.broadcasted_iota(I32, (nchunk, cpos), 0)
    lane_i = lax.broadcasted_iota(I32, (nchunk, cpos), 1)
    tok = sub_i * cpos + lane_i
    capf = jnp.float32(cap)

    def bits(e):
        return pltpu.bitcast(aff_ref[e], I32)

    def val_step(it, taus):
        bit = jnp.left_shift(jnp.int32(1), 30 - it)
        out = []
        for e in range(ne):
            cand = taus[e] | bit
            cnt = _count(bits(e) >= cand)
            out.append(jnp.where(cnt >= capf, cand, taus[e]))
        return tuple(out)

    taus = lax.fori_loop(0, 31, val_step, tuple(jnp.zeros((1, 1), I32) for _ in range(ne)))
    need = [capf - _count(bits(e) > taus[e]) for e in range(ne)]

    tbits = int(math.log2(nchunk * cpos))

    def tie_step(it, vs):
        bit = jnp.left_shift(jnp.int32(1), tbits - 1 - it)
        out = []
        for e in range(ne):
            cand = vs[e] | bit
            cnt = _count((bits(e) == taus[e]) & (tok < cand))
            out.append(jnp.where(cnt < need[e], cand, vs[e]))
        return tuple(out)

    vs = lax.fori_loop(0, tbits, tie_step, tuple(jnp.zeros((1, 1), I32) for _ in range(ne)))

    ksum = jnp.zeros((nchunk, cpos), F32)
    for e in range(ne):
        b = bits(e)
        m = ((b > taus[e]) | ((b == taus[e]) & (tok <= vs[e]))).astype(F32)
        mask_scr[e] = m
        pos_scr[e] = ksum
        ksum = ksum + m
    tri_u = (lax.broadcasted_iota(I32, (cpos, cpos), 0)
             <= lax.broadcasted_iota(I32, (cpos, cpos), 1)).astype(BF16)
    cs_i = lax.broadcasted_iota(I32, (nchunk, nchunk), 0)
    cl_i = lax.broadcasted_iota(I32, (nchunk, nchunk), 1)
    tri_ls = (cl_i < cs_i).astype(BF16)
    tri_li = (cl_i <= cs_i).astype(BF16)
    kin = jnp.dot(ksum.astype(BF16), tri_u, preferred_element_type=F32)
    ktot = jnp.broadcast_to(kin[:, cpos - 1:cpos], (nchunk, cpos))
    khi = jnp.floor(ktot * (1.0 / 256.0))
    klo = ktot - 256.0 * khi
    kstart = (256.0 * jnp.dot(tri_ls, khi.astype(BF16), preferred_element_type=F32)
              + jnp.dot(tri_ls, klo.astype(BF16), preferred_element_type=F32))
    base = kstart + kin - ksum
    base_ref[...] = base
    bend_ref[...] = base + ksum

    s_row = lax.broadcasted_iota(I32, (1, cap), 1).astype(F32)
    c_col = lax.broadcasted_iota(I32, (nchunk, cap), 0).astype(F32)
    j_col = lax.broadcasted_iota(I32, (cpos, cap), 0).astype(F32)
    reps = cap // cpos
    for e in range(ne):
        m = mask_scr[e]
        posv = pos_scr[e] + base
        lin = jnp.dot(m.astype(BF16), tri_u, preferred_element_type=F32)
        tot = jnp.broadcast_to(lin[:, cpos - 1:cpos], (nchunk, cpos))
        cend = jnp.dot(tri_li, tot.astype(BF16), preferred_element_type=F32)
        cstart = cend - tot
        cend_w = jnp.concatenate([cend] * reps, axis=1)
        cstart_w = jnp.concatenate([cstart] * reps, axis=1)
        cidx = jnp.sum((cend_w <= s_row).astype(F32), axis=0, keepdims=True)
        oh = c_col == cidx
        ohb = oh.astype(BF16)
        cst = jnp.sum(jnp.where(oh, cstart_w, 0.0), axis=0, keepdims=True)
        target = s_row - cst
        g = jnp.dot(lin.T.astype(BF16), ohb, preferred_element_type=F32)
        jidx = jnp.sum((g <= target).astype(F32), axis=0, keepdims=True)
        ohj = j_col == jidx
        idx_ref[e:e + 1, :] = (cidx * float(cpos) + jidx).astype(I32)
        a_hi, a_mid, a_lo = _split3(aff_ref[e].T)
        ga = (jnp.dot(a_hi, ohb, preferred_element_type=F32) + jnp.dot(a_mid, ohb, preferred_element_type=F32)
              + jnp.dot(a_lo, ohb, preferred_element_type=F32))
        gate_ref[e:e + 1, :] = jnp.sum(jnp.where(ohj, ga, 0.0), axis=0, keepdims=True)
        pt = posv.T
        p_hi = jnp.floor(pt * (1.0 / 256.0))
        p_lo = pt - 256.0 * p_hi
        gp = (256.0 * jnp.dot(p_hi.astype(BF16), ohb, preferred_element_type=F32)
              + jnp.dot(p_lo.astype(BF16), ohb, preferred_element_type=F32))
        pos_ref[e:e + 1, :] = jnp.sum(jnp.where(ohj, gp, 0.0), axis=0, keepdims=True).astype(I32)


def _route(aff_t, cap):
    ne, t = aff_t.shape
    nchunk = t // LANES
    aff3 = aff_t.reshape(ne, nchunk, LANES)
    return pl.pallas_call(
        functools.partial(_route_body, cap=cap, nchunk=nchunk),
        in_specs=[pl.BlockSpec(memory_space=pltpu.VMEM)],
        out_specs=[pl.BlockSpec(memory_space=pltpu.VMEM)] * 5,
        out_shape=[
            jax.ShapeDtypeStruct((ne, cap), I32),
            jax.ShapeDtypeStruct((ne, cap), F32),
            jax.ShapeDtypeStruct((ne, cap), I32),
            jax.ShapeDtypeStruct((nchunk, LANES), F32),
            jax.ShapeDtypeStruct((nchunk, LANES), F32),
        ],
        scratch_shapes=[pltpu.VMEM((ne, nchunk, LANES), F32), pltpu.VMEM((ne, nchunk, LANES), F32)],
        compiler_params=pltpu.CompilerParams(vmem_limit_bytes=VMEM_LIMIT),
        name="route",
    )(aff3)


def _ffn_body(idx_ref, idxn_ref, pos_ref, posp_ref, gate_ref, gn_ref, wg_ref, wu_ref, wd_ref, h_hbm, y_hbm,
              xbuf, ybuf, gsem, ssem, wg_s, wu_s, wd_s, *, tc, nsteps):
    e = pl.program_id(0)
    j = pl.program_id(1)
    s = e * pl.num_programs(1) + j
    slot = lax.rem(s, 2)

    def tile(ref, row):
        return ref.at[pl.ds(pl.multiple_of(row * ROW_TILE, ROW_TILE), ROW_TILE), :]

    def gather_copy(tok, i, sl):
        return pltpu.make_async_copy(tile(h_hbm, tok), tile(xbuf.at[sl], i), gsem.at[sl])

    def scatter_copy(row, i, sl):
        return pltpu.make_async_copy(tile(ybuf.at[sl], i), tile(y_hbm, row), ssem.at[sl])

    def start_gather(ref, sl):
        def body(i, c):
            gather_copy(ref[0, 0, i], i, sl).start()
            return c
        lax.fori_loop(0, tc, body, 0, unroll=8)

    def wait_scatter(sl):
        def body(i, c):
            scatter_copy(0, i, sl).wait()
            return c
        lax.fori_loop(0, tc, body, 0, unroll=8)

    def wait_gather(sl):
        def body(i, c):
            gather_copy(0, i, sl).wait()
            return c
        lax.fori_loop(0, tc, body, 0, unroll=8)

    first = s == 0

    @pl.when(first)
    def _():
        start_gather(idx_ref, 0)
        ybuf[1] = jnp.zeros(ybuf.shape[1:], F32)

    @pl.when(j == 0)
    def _():
        wg_s[...] = wg_ref[0, 0].astype(BF16)
        wu_s[...] = wu_ref[0, 0].astype(BF16)
        wd_s[...] = wd_ref[0, 0].astype(BF16)

    wait_gather(slot)
    x = _load_row_tiles(xbuf, tc, lead=(slot,))

    for i in range(tc):
        gather_copy(idxn_ref[0, 0, i], i, 1 - slot).start()
    for i in range(tc):
        scatter_copy(posp_ref[0, 0, i], i, 1 - slot).start()

    ms = jnp.mean(x * x, axis=-1, keepdims=True)
    xb = (x * lax.rsqrt(ms + RMS_EPS) * gn_ref[...]).astype(BF16)
    g = jnp.dot(xb, wg_s[...], preferred_element_type=F32)
    u = jnp.dot(xb, wu_s[...], preferred_element_type=F32)
    hid = (g * (1.0 / (1.0 + jnp.exp(-g))) * u).astype(BF16)
    y = jnp.dot(hid, wd_s[...], preferred_element_type=F32) * gate_ref[...]

    @pl.when(s >= 1)
    def _():
        wait_scatter(slot)

    _store_row_tiles(ybuf.at[slot], y)

    @pl.when(s == nsteps - 1)
    def _():
        def start_s(i, c):
            scatter_copy(pos_ref[0, 0, i], i, slot).start()
            return c
        lax.fori_loop(0, tc, start_s, 0, unroll=8)
        wait_gather(1 - slot)
        wait_scatter(1 - slot)
        wait_scatter(slot)


def _expert_ffn(x_tiles, norm_gain, idx, pos, gate_col, w_gate, w_up, w_down, layer):
    _, ne, d, f = w_gate.shape
    cap = idx.shape[1]
    tc = min(FFN_TC, cap)
    nc = cap // tc
    nsteps = ne * nc
    assert nsteps >= 2
    idx3 = idx.reshape(nsteps, 1, tc)
    pos3 = pos.reshape(nsteps, 1, tc)
    spare = (ne * cap + jnp.arange(tc, dtype=I32)).reshape(1, 1, tc)
    pos_prev3 = jnp.concatenate([spare, pos3[:-1]], axis=0)
    smem = functools.partial(pl.BlockSpec, (1, 1, tc), memory_space=pltpu.SMEM)
    return pl.pallas_call(
        functools.partial(_ffn_body, tc=tc, nsteps=nsteps),
        grid=(ne, nc),
        in_specs=[
            smem(lambda e, j: (e * nc + j, 0, 0)),
            smem(lambda e, j: (jnp.minimum(e * nc + j + 1, nsteps - 1), 0, 0)),
            smem(lambda e, j: (e * nc + j, 0, 0)),
            smem(lambda e, j: (e * nc + j, 0, 0)),
            pl.BlockSpec((tc, 1), lambda e, j: (e * nc + j, 0)),
            pl.BlockSpec((1, d), lambda e, j: (0, 0)),
            pl.BlockSpec((1, 1, d, f), lambda e, j: (layer, e, 0, 0)),
            pl.BlockSpec((1, 1, d, f), lambda e, j: (layer, e, 0, 0)),
            pl.BlockSpec((1, 1, f, d), lambda e, j: (layer, e, 0, 0)),
            pl.BlockSpec(memory_space=pl.ANY),
        ],
        out_specs=pl.BlockSpec(memory_space=pl.ANY),
        out_shape=jax.ShapeDtypeStruct(((ne * cap + tc) * ROW_TILE, LANES), F32),
        scratch_shapes=[
            pltpu.VMEM((2, tc * ROW_TILE, LANES), F32),
            pltpu.VMEM((2, tc * ROW_TILE, LANES), F32),
            pltpu.SemaphoreType.DMA((2,)),
            pltpu.SemaphoreType.DMA((2,)),
            pltpu.VMEM((d, f), BF16), pltpu.VMEM((d, f), BF16), pltpu.VMEM((f, d), BF16),
        ],
        compiler_params=_cparams(("arbitrary", "arbitrary")),
        name="expert_ffn",
    )(idx3, idx3, pos3, pos_prev3, gate_col, norm_gain, w_gate, w_up, w_down, x_tiles)


def _combine_body(tile_ref, chunk_ref, valid_ref, y_ref, base_ref, bend_ref, x_ref, o_ref):
    i = pl.program_id(0)
    t = tile_ref[i]
    prev = tile_ref[jnp.maximum(i - 1, 0)]

    @pl.when((i == 0) | (prev != t))
    def _():
        o_ref[...] = _load_row_tiles(x_ref, o_ref.shape[0])

    @pl.when(valid_ref[i] != 0)
    def _():
        tt = o_ref.shape[0]
        rows = y_ref.shape[0] // ROW_TILE
        y = _load_row_tiles(y_ref, rows)
        a = (lax.broadcasted_iota(I32, (rows, tt), 0) + chunk_ref[i] * rows).astype(F32)
        pt = ((a >= base_ref[0]).astype(F32) - (a >= bend_ref[0]).astype(F32)).astype(BF16)
        y_hi = y.astype(BF16)
        y_lo = (y - y_hi.astype(F32)).astype(BF16)
        o_ref[...] += lax.dot_general(jnp.concatenate([pt, pt], axis=0), jnp.concatenate([y_hi, y_lo], axis=0),
                                      (((0,), (0,)), ((), ())), preferred_element_type=F32)


def _combine(y_tiles, base, bend, x_tiles, tile_id, chunk_id, valid):
    t = base.shape[0]
    d = ROW_TILE * LANES
    n_items = tile_id.shape[0]
    ntiles = t // SEG_TT
    row_spec = pl.BlockSpec((1, 1, SEG_TT), lambda i, tl, ch, va: (tl[i], 0, 0))
    grid_spec = pltpu.PrefetchScalarGridSpec(
        num_scalar_prefetch=3,
        grid=(n_items,),
        in_specs=[
            pl.BlockSpec((SEG_CH * ROW_TILE, LANES), lambda i, tl, ch, va: (ch[i], 0)),
            row_spec,
            row_spec,
            pl.BlockSpec((SEG_TT * ROW_TILE, LANES), lambda i, tl, ch, va: (tl[i], 0)),
        ],
        out_specs=pl.BlockSpec((SEG_TT, d), lambda i, tl, ch, va: (tl[i], 0)),
    )
    return pl.pallas_call(
        _combine_body,
        grid_spec=grid_spec,
        out_shape=jax.ShapeDtypeStruct((t, d), F32),
        compiler_params=_cparams(("arbitrary",)),
        name="combine",
    )(tile_id, chunk_id, valid, y_tiles, base.reshape(ntiles, 1, SEG_TT), bend.reshape(ntiles, 1, SEG_TT), x_tiles)


def _combine_plan(base_flat, nrows):
    t = base_flat.shape[0]
    ntiles = t // SEG_TT
    nchunks = nrows // SEG_CH
    n_items = ntiles + nchunks
    b = jnp.concatenate([base_flat[::SEG_TT].astype(I32), jnp.full((1,), nrows, I32)])
    lo = jnp.minimum(b[:-1] // SEG_CH, nchunks - 1)
    hi = jnp.maximum((b[1:] - 1) // SEG_CH, lo)
    cnt = hi - lo + 1
    ends = jnp.cumsum(cnt)
    starts = ends - cnt
    item = jnp.arange(n_items, dtype=I32)
    tile = jnp.minimum(jnp.searchsorted(ends, item, side="right").astype(I32), ntiles - 1)
    chunk = jnp.minimum(lo[tile] + item - starts[tile], nchunks - 1)
    valid = (item < ends[-1]).astype(I32)
    return tile, chunk, valid


def _moe(x_tiles, aff_t, norm_gain, w_gate, w_up, w_down, layer):
    t = aff_t.shape[1]
    cap = EC_CAPACITY_FACTOR * t // N_EXPERTS
    idx, gate, pos, base, bend = _route(aff_t, cap)
    ys = _expert_ffn(x_tiles, norm_gain, idx, pos, gate.reshape(-1, 1), w_gate, w_up, w_down, layer)
    base, bend = base.reshape(-1), bend.reshape(-1)
    tile, chunk, valid = _combine_plan(base, N_EXPERTS * cap)
    return _combine(ys, base, bend, x_tiles, tile, chunk, valid)


def _block_diag_ones():
    i = np.arange(CHUNK) // HEAD_DIM
    return jnp.asarray((i[:, None] == i[None, :]).astype(np.float32), dtype=BF16)


def _router_split(w_router):
    w = jnp.pad(w_router.astype(F32), ((0, 0), (0, LANES - N_EXPERTS)))
    hi = w.astype(BF16)
    lo = (w - hi.astype(F32)).astype(BF16)
    return hi, lo


def _layer_a(x, rel_bias, norm_mix, w_in, q_gain, k_gain, w_out, norm_ffn, router):
    bsz, seq, d = x.shape
    qk_gain = jnp.concatenate([jnp.tile(q_gain.astype(F32), N_HEADS) * (HEAD_DIM ** -0.5),
                               jnp.tile(k_gain.astype(F32), N_HEADS),
                               jnp.ones((MIX_WIDTH,), F32)])[None]
    nq = MIX_WIDTH // CHUNK
    groups = tuple((dil, (g, nq + g, 2 * nq + g)) for g, (_, dil) in enumerate(DILATED_GROUPS))
    qkv = _project(x, norm_mix[None].astype(F32), w_in.astype(BF16), _block_diag_ones(), qk_gain,
                   n_norm_chunks=2 * nq, groups=groups)
    o_list, lse_list = [], []
    for g, (window, dil) in enumerate(DILATED_GROUPS):
        half = window // (2 * dil)
        heads = list(range(g * HEADS_PER_CHUNK, (g + 1) * HEADS_PER_CHUNK))
        bias = _bias_table(rel_bias, heads, half, min(3 * half, seq // dil), dil)
        o, lse = _banded_attention(qkv[g], qkv[g], bias, None, qb=half, qsub=half, gq=1, q_col=0, k_col=1, v_col=2,
                                   out_dtype=F32, want_lse=True, unroll=ATTN_UNROLL_A)
        o_list.append(o)
        lse_list.append(lse)
    r_hi, r_lo = _router_split(router)
    return _out_project(o_list, lse_list, x, w_out.astype(BF16), norm_ffn[None].astype(F32), r_hi, r_lo)


def _gqa_perm():
    g, kv, dd = np.meshgrid(np.arange(Q_PER_KV), np.arange(N_KV_HEADS), np.arange(HEAD_DIM), indexing="ij")
    return (kv * Q_PER_KV * HEAD_DIM + g * HEAD_DIM + dd).reshape(-1)


def _layer_b(x, rel_bias, norm_mix, w_in, q_gain, k_gain, sink, w_out, norm_ffn, router):
    bsz, seq, d = x.shape
    perm = _gqa_perm()
    w_in_p = jnp.concatenate([w_in[:, :MIX_WIDTH][:, perm], w_in[:, MIX_WIDTH:]], axis=1).astype(BF16)
    w_out_p = w_out[perm, :].astype(BF16)
    qk_gain = jnp.concatenate([jnp.tile(q_gain.astype(F32), N_HEADS) * (HEAD_DIM ** -0.5),
                               jnp.tile(k_gain.astype(F32), N_KV_HEADS),
                               jnp.ones((KV_WIDTH,), F32)])[None]
    nchunks = (MIX_WIDTH + 2 * KV_WIDTH) // CHUNK
    (qkv,) = _project(x, norm_mix[None].astype(F32), w_in_p, _block_diag_ones(), qk_gain,
                      n_norm_chunks=(MIX_WIDTH + KV_WIDTH) // CHUNK, groups=((1, tuple(range(nchunks))),))
    half = LOCAL_HALF_WINDOW
    heads = [kv * Q_PER_KV + g for g in range(Q_PER_KV) for kv in range(N_KV_HEADS)]
    bias = _bias_table(rel_bias, heads, half, min(3 * half, seq), 1)
    sink_tab = jnp.repeat(sink.astype(F32)[np.asarray(heads)].reshape(Q_PER_KV, N_KV_HEADS), ATTN_QSUB_B,
                          axis=1)[..., None]
    (o,) = _banded_attention(qkv, qkv, bias, sink_tab, qb=half, qsub=ATTN_QSUB_B, gq=Q_PER_KV, q_col=0,
                             k_col=MIX_WIDTH // CHUNK,
                             v_col=MIX_WIDTH // CHUNK + 1, out_dtype=BF16, want_lse=False, unroll=ATTN_UNROLL_B)
    r_hi, r_lo = _router_split(router)
    return _out_project([o], [], x, w_out_p, norm_ffn[None].astype(F32), r_hi, r_lo)


def _trunk(x, rel_bias, norm_mix, norm_ffn, a_w_in, a_q_gain, a_k_gain, a_w_out, b_w_in, b_q_gain, b_k_gain,
           b_sink, b_w_out, moe_router, moe_w_gate, moe_w_up, moe_w_down):
    shape = x.shape
    for i in range(DEPTH):
        j = i // 2
        if i % 2 == 0:
            x_tiles, aff_t = _layer_a(x, rel_bias, norm_mix[i], a_w_in[j], a_q_gain[j], a_k_gain[j], a_w_out[j],
                                      norm_ffn[i], moe_router[i])
        else:
            x_tiles, aff_t = _layer_b(x, rel_bias, norm_mix[i], b_w_in[j], b_q_gain[j], b_k_gain[j], b_sink[j],
                                      b_w_out[j], norm_ffn[i], moe_router[i])
        x = _moe(x_tiles, aff_t, norm_ffn[i][None].astype(F32), moe_w_gate, moe_w_up, moe_w_down, i).reshape(shape)
    return x


def kernel(x_prompt, x_sample, rel_bias, norm_mix, norm_ffn, a_w_in, a_q_gain, a_k_gain, a_w_out, b_w_in, b_q_gain, b_k_gain, b_sink, b_w_out, moe_router, moe_w_gate, moe_w_up, moe_w_down):
    weights = (rel_bias, norm_mix, norm_ffn, a_w_in, a_q_gain, a_k_gain, a_w_out, b_w_in, b_q_gain, b_k_gain,
               b_sink, b_w_out, moe_router, moe_w_gate, moe_w_up, moe_w_down)
    return _trunk(x_prompt, *weights), _trunk(x_sample, *weights)
```
